```python
import math
import jax
import jax.numpy as jnp
from jax import lax
import numpy as np

D_MODEL = 2048
BATCH = 4
SEQ = 2048
DEPTH = 2

EPS = 1e-6
POOL_WINDOWS = (2, 4, 8, 16)
POOL_WIDTH = D_MODEL // 2
POOL_GROUP = POOL_WIDTH // len(POOL_WINDOWS)
HEAD_DIM = 64
N_Q_HEADS = (D_MODEL // 2) // HEAD_DIM
N_KV_HEADS = N_Q_HEADS // 4
Q_PER_KV = N_Q_HEADS // N_KV_HEADS
ATTN_WIDTH = N_Q_HEADS * HEAD_DIM
KV_WIDTH = N_KV_HEADS * HEAD_DIM
WINDOW = 128
BLOCK = 128
MIX_IN_WIDTH = POOL_WIDTH + ATTN_WIDTH + 2 * KV_WIDTH
MIX_OUT_WIDTH = POOL_WIDTH + ATTN_WIDTH
D_INNER = 2 * D_MODEL
SSM_HEAD_DIM = 64
N_SSM_HEADS = D_INNER // SSM_HEAD_DIM
N_SSM_GROUPS = 8
HEADS_PER_GROUP = N_SSM_HEADS // N_SSM_GROUPS
D_STATE = 128
CONV_WIDTH = 4
CHUNK = 128
CONV_CH = D_INNER + 2 * N_SSM_GROUPS * D_STATE
SSM_IN_WIDTH = D_INNER + CONV_CH + N_SSM_HEADS
N_EXPERT_GROUPS = 4
EXPERTS_PER_GROUP = 4
N_EXPERTS = N_EXPERT_GROUPS * EXPERTS_PER_GROUP
TOP_K_IN_GROUP = 2
D_FF_EXPERT = D_MODEL // 4

N_EVEN = (DEPTH + 1) // 2
N_ODD = DEPTH // 2

kernel_name = "hybrid_pool_swa_ssd_hmoe"


def rms_norm(x, w):
    xf = x.astype(jnp.float32)
    var = jnp.mean(xf * xf, axis=-1, keepdims=True)
    return (xf * lax.rsqrt(var + EPS)).astype(x.dtype) * w


def causal_pool_mixer(u, pool_w, pool_scale):
    s = u.shape[1]
    uf = u.astype(jnp.float32)
    csum = jnp.pad(jnp.cumsum(uf, axis=1), ((0, 0), (1, 0), (0, 0)))
    pos = jnp.arange(s)
    outs = []
    for g, w in enumerate(POOL_WINDOWS):
        sl = slice(g * POOL_GROUP, (g + 1) * POOL_GROUP)
        c = csum[:, :, sl]
        lo = jnp.maximum(pos + 1 - w, 0)
        count = (pos + 1 - lo).astype(jnp.float32)[None, :, None]
        mean = (jnp.take(c, pos + 1, axis=1) - jnp.take(c, lo, axis=1)) / count
        pooled = (mean - uf[:, :, sl]).astype(u.dtype)
        outs.append(jnp.einsum("bsc,cd->bsd", pooled, pool_w[g]))
    return jnp.concatenate(outs, axis=-1) * pool_scale


def sliding_window_gqa(q, k, v, sinks, slopes):
    b, s = q.shape[:2]
    nb = s // BLOCK
    qb = q.reshape(b, nb, BLOCK, N_KV_HEADS, Q_PER_KV, HEAD_DIM)
    kb = k.reshape(b, nb, BLOCK, N_KV_HEADS, HEAD_DIM)
    vb = v.reshape(b, nb, BLOCK, N_KV_HEADS, HEAD_DIM)
    pad = ((0, 0), (1, 0), (0, 0), (0, 0), (0, 0))
    kk = jnp.concatenate([jnp.pad(kb, pad)[:, :-1], kb], axis=2)
    vv = jnp.concatenate([jnp.pad(vb, pad)[:, :-1], vb], axis=2)
    scores = jnp.einsum("bnqgrd,bnkgd->bngrqk", qb, kk,
                        preferred_element_type=jnp.float32) * (HEAD_DIM ** -0.5)
    qpos = jnp.arange(BLOCK)[:, None] + BLOCK
    kpos = jnp.arange(2 * BLOCK)[None, :]
    dist = qpos - kpos
    band = (dist >= 0) & (dist < WINDOW)
    key_exists = (jnp.arange(nb)[:, None] > 0) | (kpos >= BLOCK)
    mask = band[None] & key_exists[:, None, :]
    alibi = -slopes.reshape(N_KV_HEADS, Q_PER_KV)[:, :, None, None] * dist.astype(jnp.float32)
    scores = jnp.where(mask[None, :, None, None], scores + alibi[None, None], -jnp.inf)
    sink = sinks.astype(jnp.float32).reshape(N_KV_HEADS, Q_PER_KV)[None, None, :, :, None, None]
    m = jnp.maximum(jnp.max(scores, axis=-1, keepdims=True), sink)
    p = jnp.exp(scores - m)
    probs = p / (jnp.sum(p, axis=-1, keepdims=True) + jnp.exp(sink - m))
    out = jnp.einsum("bngrqk,bnkgd->bnqgrd", probs.astype(v.dtype), vv)
    return out.reshape(b, s, ATTN_WIDTH)


def pool_attention_mixer(h, w_in, pool_w, pool_scale, sinks, w_out, slopes):
    b, s, _ = h.shape
    proj = h @ w_in
    u, q, k, v = jnp.split(proj, [POOL_WIDTH, POOL_WIDTH + ATTN_WIDTH,
                                  POOL_WIDTH + ATTN_WIDTH + KV_WIDTH], axis=-1)
    pooled = causal_pool_mixer(u, pool_w, pool_scale)
    attn = sliding_window_gqa(q.reshape(b, s, N_Q_HEADS, HEAD_DIM),
                              k.reshape(b, s, N_KV_HEADS, HEAD_DIM),
                              v.reshape(b, s, N_KV_HEADS, HEAD_DIM), sinks, slopes)
    return jnp.concatenate([pooled, attn], axis=-1) @ w_out


def ssd_chunked_scan(x, dt, a, b_ssm, c_ssm):
    f32 = jnp.float32
    bsz, s = x.shape[:2]
    nc = s // CHUNK
    g, r, p, n = N_SSM_GROUPS, HEADS_PER_GROUP, SSM_HEAD_DIM, D_STATE
    dt = dt.reshape(bsz, s, g, r)
    x_dt = (x.astype(f32) * dt[..., None]).reshape(bsz, nc, CHUNK, g, r, p)
    bc = b_ssm.astype(f32).reshape(bsz, nc, CHUNK, g, n)
    cc = c_ssm.astype(f32).reshape(bsz, nc, CHUNK, g, n)
    da = (dt * a.reshape(g, r)).reshape(bsz, nc, CHUNK, g, r)
    a_cum = jnp.cumsum(jnp.moveaxis(da, 2, -1), axis=-1)
    causal = jnp.tril(jnp.ones((CHUNK, CHUNK), dtype=bool))
    seg = a_cum[..., :, None] - a_cum[..., None, :]
    decay_ls = jnp.exp(jnp.where(causal, seg, -jnp.inf))
    cb = jnp.einsum("bclgn,bcsgn->bcgls", cc, bc)
    y_diag = jnp.einsum("bcgls,bcgrls,bcsgrp->bclgrp", cb, decay_ls, x_dt)
    decay_to_end = jnp.exp(a_cum[..., -1:] - a_cum)
    chunk_states = jnp.einsum("bcsgn,bcgrs,bcsgrp->bcgrpn", bc, decay_to_end, x_dt)
    chunk_decay = jnp.exp(a_cum[..., -1])

    def step(state, inputs):
        st, dec = inputs
        return state * dec[..., None, None] + st, state

    init = jnp.zeros((bsz, g, r, p, n), f32)
    _, prev = lax.scan(step, init, (jnp.moveaxis(chunk_states, 1, 0),
                                    jnp.moveaxis(chunk_decay, 1, 0)))
    prev = jnp.moveaxis(prev, 0, 1)
    y_off = jnp.einsum("bclgn,bcgrpn,bcgrl->bclgrp", cc, prev, jnp.exp(a_cum))
    return (y_diag + y_off).reshape(bsz, s, g, r, p)


def mamba2_mixer(h, w_in, conv_w, conv_b, dt_bias, a_log, d_skip, norm_w, w_out):
    f32 = jnp.float32
    b, s, _ = h.shape
    proj = h @ w_in
    z, xbc, dt_raw = jnp.split(proj, [D_INNER, D_INNER + CONV_CH], axis=-1)
    xbc = lax.conv_general_dilated(xbc, conv_w[:, None, :], window_strides=(1,),
                                   padding=[(CONV_WIDTH - 1, 0)],
                                   dimension_numbers=("NWC", "WIO", "NWC"),
                                   feature_group_count=CONV_CH)
    xbc = jax.nn.silu(xbc + conv_b)
    xs, b_ssm, c_ssm = jnp.split(xbc, [D_INNER, D_INNER + N_SSM_GROUPS * D_STATE], axis=-1)
    dt = jax.nn.softplus(dt_raw.astype(f32) + dt_bias.astype(f32))
    a = -jnp.exp(a_log.astype(f32))
    xs = xs.reshape(b, s, N_SSM_GROUPS, HEADS_PER_GROUP, SSM_HEAD_DIM)
    y = ssd_chunked_scan(xs, dt, a,
                         b_ssm.reshape(b, s, N_SSM_GROUPS, D_STATE),
                         c_ssm.reshape(b, s, N_SSM_GROUPS, D_STATE))
    y = y + d_skip.astype(f32).reshape(N_SSM_GROUPS, HEADS_PER_GROUP)[:, :, None] * xs.astype(f32)
    gshape = (b, s, N_SSM_GROUPS, D_INNER // N_SSM_GROUPS)
    gated = y.reshape(gshape) * jax.nn.silu(z.astype(f32)).reshape(gshape)
    var = jnp.mean(gated * gated, axis=-1, keepdims=True)
    normed = (gated * lax.rsqrt(var + EPS)).reshape(b, s, D_INNER).astype(h.dtype) * norm_w
    return normed @ w_out


def hierarchical_moe(h, w_group, b_group, w_router, b_router, w_gate, w_up, w_down):
    f32 = jnp.float32
    b, s, d = h.shape
    t = h.reshape(b * s, d)
    g_logits = (t @ w_group).astype(f32) + b_group.astype(f32)
    g_probs = jax.nn.softmax(g_logits, axis=-1)
    g_val, g_idx = lax.top_k(g_logits, 1)
    g_weight = jnp.take_along_axis(g_probs, g_idx, axis=-1)
    e_all = jnp.einsum("td,gde->tge", t, w_router).astype(f32) + b_router.astype(f32)
    e_logits = jnp.take_along_axis(e_all, g_idx[:, :, None], axis=1)[:, 0]
    top_vals, top_idx = lax.top_k(e_logits, TOP_K_IN_GROUP)
    top_w = jax.nn.softmax(top_vals, axis=-1) * g_weight
    expert_id = g_idx * EXPERTS_PER_GROUP + top_idx
    gates = jnp.sum(jax.nn.one_hot(expert_id, N_EXPERTS, dtype=f32) * top_w[..., None], axis=1)
    a_gate = jnp.einsum("td,edf->tef", t, w_gate)
    a_up = jnp.einsum("td,edf->tef", t, w_up)
    act = jax.nn.silu(a_gate) * a_up * gates[..., None].astype(t.dtype)
    out = jnp.einsum("tef,efd->td", act, w_down)
    return out.reshape(b, s, d)


def setup_inputs(seed: int = 0) -> dict:
    key = jax.random.key(seed)
    ks = jax.random.split(key, 32)
    f32 = jnp.float32

    def nrm(k, shape, scale):
        return jax.random.normal(k, shape, f32) * scale

    dt0 = jnp.exp(jax.random.uniform(ks[14], (N_ODD, N_SSM_HEADS), f32,
                                     math.log(1e-3), math.log(1e-1)))
    return {
        "x": nrm(ks[0], (BATCH, SEQ, D_MODEL), 1.0),
        "norm_mix_w": 1.0 + nrm(ks[1], (DEPTH, D_MODEL), 0.02),
        "norm_ffn_w": 1.0 + nrm(ks[2], (DEPTH, D_MODEL), 0.02),
        "final_norm_w": 1.0 + nrm(ks[3], (D_MODEL,), 0.02),
        "mix_w_in": nrm(ks[4], (N_EVEN, D_MODEL, MIX_IN_WIDTH), D_MODEL ** -0.5),
        "pool_w": nrm(ks[5], (N_EVEN, len(POOL_WINDOWS), POOL_GROUP, POOL_GROUP), POOL_GROUP ** -0.5),
        "pool_scale": 1.0 + nrm(ks[6], (N_EVEN, POOL_WIDTH), 0.02),
        "attn_sinks": nrm(ks[7], (N_EVEN, N_Q_HEADS), 1.0),
        "mix_w_out": nrm(ks[8], (N_EVEN, MIX_OUT_WIDTH, D_MODEL), MIX_OUT_WIDTH ** -0.5),
        "ssm_w_in": nrm(ks[9], (N_ODD, D_MODEL, SSM_IN_WIDTH), D_MODEL ** -0.5),
        "ssm_conv_w": nrm(ks[10], (N_ODD, CONV_WIDTH, CONV_CH), CONV_WIDTH ** -0.5),
        "ssm_conv_b": nrm(ks[11], (N_ODD, CONV_CH), 0.01),
        "ssm_dt_bias": dt0 + jnp.log(-jnp.expm1(-dt0)),
        "ssm_a_log": jnp.log(jax.random.uniform(ks[12], (N_ODD, N_SSM_HEADS), f32, 1.0, 16.0)),
        "ssm_d": 1.0 + nrm(ks[13], (N_ODD, N_SSM_HEADS), 0.01),
        "ssm_norm_w": 1.0 + nrm(ks[15], (N_ODD, D_INNER), 0.02),
        "ssm_w_out": nrm(ks[16], (N_ODD, D_INNER, D_MODEL), D_INNER ** -0.5),
        "moe_w_group": nrm(ks[17], (DEPTH, D_MODEL, N_EXPERT_GROUPS), D_MODEL ** -0.5),
        "moe_b_group": nrm(ks[18], (DEPTH, N_EXPERT_GROUPS), 0.01),
        "moe_w_router": nrm(ks[19], (DEPTH, N_EXPERT_GROUPS, D_MODEL, EXPERTS_PER_GROUP), D_MODEL ** -0.5),
        "moe_b_router": nrm(ks[20], (DEPTH, N_EXPERT_GROUPS, EXPERTS_PER_GROUP), 0.01),
        "moe_w_gate": nrm(ks[21], (DEPTH, N_EXPERTS, D_MODEL, D_FF_EXPERT), D_MODEL ** -0.5),
        "moe_w_up": nrm(ks[22], (DEPTH, N_EXPERTS, D_MODEL, D_FF_EXPERT), D_MODEL ** -0.5),
        "moe_w_down": nrm(ks[23], (DEPTH, N_EXPERTS, D_FF_EXPERT, D_MODEL), D_FF_EXPERT ** -0.5),
    }


def reference(x, norm_mix_w, norm_ffn_w, final_norm_w, mix_w_in, pool_w, pool_scale,
              attn_sinks, mix_w_out, ssm_w_in, ssm_conv_w, ssm_conv_b, ssm_dt_bias,
              ssm_a_log, ssm_d, ssm_norm_w, ssm_w_out, moe_w_group, moe_b_group,
              moe_w_router, moe_b_router, moe_w_gate, moe_w_up, moe_w_down):
    slopes = jnp.asarray(2.0 ** (-8.0 * np.arange(1, N_Q_HEADS + 1) / N_Q_HEADS), dtype=jnp.float32)
    h = x
    for layer in range(DEPTH):
        i = layer // 2
        hn = rms_norm(h, norm_mix_w[layer])
        if layer % 2 == 0:
            mix = pool_attention_mixer(hn, mix_w_in[i], pool_w[i], pool_scale[i],
                                       attn_sinks[i], mix_w_out[i], slopes)
        else:
            mix = mamba2_mixer(hn, ssm_w_in[i], ssm_conv_w[i], ssm_conv_b[i], ssm_dt_bias[i],
                               ssm_a_log[i], ssm_d[i], ssm_norm_w[i], ssm_w_out[i])
        h = h + mix
        h = h + hierarchical_moe(rms_norm(h, norm_ffn_w[layer]), moe_w_group[layer],
                                 moe_b_group[layer], moe_w_router[layer], moe_b_router[layer],
                                 moe_w_gate[layer], moe_w_up[layer], moe_w_down[layer])
    return rms_norm(h, final_norm_w)
```

```python
import functools
import math

import jax
import jax.numpy as jnp
import numpy as np
from jax import lax
from jax.experimental import pallas as pl
from jax.experimental.pallas import tpu as pltpu

F32 = jnp.float32
BF16 = jnp.bfloat16
I32 = jnp.int32

D_MODEL = 2048
EPS = 1e-6
POOL_WINDOWS = (2, 4, 8, 16)
POOL_WIDTH = 1024
POOL_GROUP = 256
HEAD_DIM = 64
N_Q_HEADS = 16
N_KV_HEADS = 4
Q_PER_KV = 4
ATTN_WIDTH = 1024
KV_WIDTH = 256
WINDOW = 128
BLOCK = 128
MIX_IN_WIDTH = 2560
D_INNER = 4096
SSM_HEAD_DIM = 64
N_SSM_HEADS = 64
N_SSM_GROUPS = 8
HEADS_PER_GROUP = 8
D_STATE = 128
CONV_WIDTH = 4
CHUNK = 128
CONV_CH = D_INNER + 2 * N_SSM_GROUPS * D_STATE
ZX_WIDTH = D_INNER + CONV_CH
N_EXPERT_GROUPS = 4
EXPERTS_PER_GROUP = 4
N_EXPERTS = 16
D_FF_EXPERT = 512
SLOPES = tuple(float(2.0 ** (-8.0 * (i + 1) / N_Q_HEADS)) for i in range(N_Q_HEADS))

V7X_LANES = 128
V7X_SUBLANES = 8
V7X_VMEM_BYTES = 64 * 1024 * 1024

PROJ_TM = 1024
PROJ_TN = 512
NORM_ROWS = 128
ROUTE_TM = 512
EXPERT_TM = 256
COMBINE_TM = 256
DISPATCH_TM = 1024
ROUTE_ROWS = 32
EXPERT_ROW0 = 8


def _cparams(n_axes, vmem_mb):
    return pltpu.CompilerParams(
        dimension_semantics=("arbitrary",) * n_axes,
        vmem_limit_bytes=vmem_mb * 1024 * 1024,
    )


def _silu(x):
    return x / (1.0 + jnp.exp(-x))


def _split_bf16(x):
    hi = x.astype(BF16)
    lo = (x - hi.astype(F32)).astype(BF16)
    return hi, lo


def _norm_matmul_kernel(x_ref, nw_ref, w_ref, o_ref, xn_ref):
    tm = x_ref.shape[0]

    @pl.when(pl.program_id(1) == 0)
    def _():
        def body(r, carry):
            rows = pl.ds(pl.multiple_of(r * NORM_ROWS, NORM_ROWS), NORM_ROWS)
            x = x_ref[rows, :]
            var = jnp.mean(x * x, axis=-1, keepdims=True)
            xn_ref[rows, :] = ((x * lax.rsqrt(var + EPS)) * nw_ref[...]).astype(BF16)
            return carry

        lax.fori_loop(0, tm // NORM_ROWS, body, 0)

    o_ref[...] = jnp.dot(xn_ref[...], w_ref[...].astype(BF16),
                         preferred_element_type=F32).astype(o_ref.dtype)


def _norm_matmul(x, nw, w, n_out, tn, out_dtype):
    t, d = x.shape
    tm = min(PROJ_TM, t)
    return pl.pallas_call(
        _norm_matmul_kernel,
        grid=(t // tm, n_out // tn),
        in_specs=[
            pl.BlockSpec((tm, d), lambda i, j: (i, 0)),
            pl.BlockSpec((1, d), lambda i, j: (0, 0)),
            pl.BlockSpec((d, tn), lambda i, j: (0, j)),
        ],
        out_specs=pl.BlockSpec((tm, tn), lambda i, j: (i, j)),
        out_shape=jax.ShapeDtypeStruct((t, n_out), out_dtype),
        scratch_shapes=[pltpu.VMEM((tm, d), BF16)],
        compiler_params=_cparams(2, 48),
        name="norm_matmul",
    )(x, nw.reshape(1, d), w)


def _matmul_res_kernel(a_ref, w_ref, r_ref, o_ref):
    o_ref[...] = r_ref[...] + jnp.dot(a_ref[...], w_ref[...].astype(BF16),
                                      preferred_element_type=F32)


def _matmul_residual(a, w, res, tn):
    t, k = a.shape
    n = w.shape[1]
    tm = min(PROJ_TM, t)
    return pl.pallas_call(
        _matmul_res_kernel,
        grid=(t // tm, n // tn),
        in_specs=[
            pl.BlockSpec((tm, k), lambda i, j: (i, 0)),
            pl.BlockSpec((k, tn), lambda i, j: (0, j)),
            pl.BlockSpec((tm, tn), lambda i, j: (i, j)),
        ],
        out_specs=pl.BlockSpec((tm, tn), lambda i, j: (i, j)),
        out_shape=jax.ShapeDtypeStruct((t, n), F32),
        compiler_params=_cparams(2, 48),
        name="matmul_residual",
    )(a, w, res)


def _pool_attn_kernel(sinks_ref, u_ref, up_ref, q_ref, k_ref, kp_ref, v_ref, vp_ref,
                      pw_ref, ps_ref, o_ref, *, blocks_per_seq):
    blk = pl.program_id(0) % blocks_per_seq
    first = blk == 0
    row = lax.broadcasted_iota(I32, (BLOCK, 2 * BLOCK), 0)
    col = lax.broadcasted_iota(I32, (BLOCK, 2 * BLOCK), 1)
    dist = row + BLOCK - col

    u_cur = u_ref[...]
    u_prev = jnp.where(first, jnp.zeros_like(u_cur), up_ref[...])
    u_ext = jnp.concatenate([u_prev, u_cur], axis=0)
    pos = blk * BLOCK + lax.broadcasted_iota(I32, (BLOCK, 1), 0)
    for g, w in enumerate(POOL_WINDOWS):
        cols = slice(g * POOL_GROUP, (g + 1) * POOL_GROUP)
        band = jnp.where((dist >= 0) & (dist < w), 1.0, 0.0).astype(BF16)
        wsum = jnp.dot(band, u_ext[:, cols], preferred_element_type=F32)
        count = jnp.minimum(pos + 1, w).astype(F32)
        pooled = (wsum / count - u_cur[:, cols].astype(F32)).astype(BF16)
        out_g = jnp.dot(pooled, pw_ref[g].astype(BF16), preferred_element_type=F32)
        o_ref[:, cols] = (out_g * ps_ref[:, cols]).astype(o_ref.dtype)

    kk = jnp.concatenate([kp_ref[...], k_ref[...]], axis=0)
    vv = jnp.concatenate([vp_ref[...], v_ref[...]], axis=0)
    valid = (dist >= 0) & (dist < WINDOW) & ((col >= BLOCK) | jnp.logical_not(first))
    dist_f = dist.astype(F32)
    scale = HEAD_DIM ** -0.5
    for pair in range(N_Q_HEADS // 2):
        q_pair = q_ref[:, pair * 2 * HEAD_DIM:(pair + 1) * 2 * HEAD_DIM]
        outs = []
        for sub in range(2):
            h = pair * 2 + sub
            g = h // Q_PER_KV
            qh = q_pair[:, sub * HEAD_DIM:(sub + 1) * HEAD_DIM]
            kg = kk[:, g * HEAD_DIM:(g + 1) * HEAD_DIM]
            vg = vv[:, g * HEAD_DIM:(g + 1) * HEAD_DIM]
            s = lax.dot_general(qh, kg, (((1,), (1,)), ((), ())),
                                preferred_element_type=F32) * scale
            s = jnp.where(valid, s - SLOPES[h] * dist_f, -jnp.inf)
            sink = sinks_ref[h]
            m = jnp.maximum(jnp.max(s, axis=-1, keepdims=True), sink)
            p = jnp.exp(s - m)
            denom = jnp.sum(p, axis=-1, keepdims=True) + jnp.exp(sink - m)
            probs = (p / denom).astype(BF16)
            outs.append(jnp.dot(probs, vg, preferred_element_type=F32))
        lo = ATTN_WIDTH + pair * 2 * HEAD_DIM
        o_ref[:, lo:lo + 2 * HEAD_DIM] = jnp.concatenate(outs, axis=1).astype(o_ref.dtype)


def _pool_attn(proj, pool_w, pool_scale, sinks, seq_len):
    t = proj.shape[0]
    nblk = t // BLOCK
    kcol = (POOL_WIDTH + ATTN_WIDTH) // KV_WIDTH

    def cur(cb):
        return lambda i: (i, cb)

    def prev(cb):
        return lambda i: (jnp.maximum(i - 1, 0), cb)

    return pl.pallas_call(
        functools.partial(_pool_attn_kernel, blocks_per_seq=seq_len // BLOCK),
        grid=(nblk,),
        in_specs=[
            pl.BlockSpec(memory_space=pltpu.SMEM),
            pl.BlockSpec((BLOCK, POOL_WIDTH), cur(0)),
            pl.BlockSpec((BLOCK, POOL_WIDTH), prev(0)),
            pl.BlockSpec((BLOCK, ATTN_WIDTH), cur(1)),
            pl.BlockSpec((BLOCK, KV_WIDTH), cur(kcol)),
            pl.BlockSpec((BLOCK, KV_WIDTH), prev(kcol)),
            pl.BlockSpec((BLOCK, KV_WIDTH), cur(kcol + 1)),
            pl.BlockSpec((BLOCK, KV_WIDTH), prev(kcol + 1)),
            pl.BlockSpec((len(POOL_WINDOWS), POOL_GROUP, POOL_GROUP), lambda i: (0, 0, 0)),
            pl.BlockSpec((1, POOL_WIDTH), lambda i: (0, 0)),
        ],
        out_specs=pl.BlockSpec((BLOCK, POOL_WIDTH + ATTN_WIDTH), lambda i: (i, 0)),
        out_shape=jax.ShapeDtypeStruct((t, POOL_WIDTH + ATTN_WIDTH), BF16),
        compiler_params=_cparams(1, 32),
        name="pool_attn",
    )(sinks, proj, proj, proj, proj, proj, proj, proj, pool_w, pool_scale.reshape(1, POOL_WIDTH))


def _ssd_kernel(z_ref, x_ref, bc_ref, dt_ref, cw_ref, cb_ref, dtb_ref, alog_ref, dexp_ref,
                nw_ref, e2_ref, o_ref, ext_ref, state_ref, y_ref):
    halo = V7X_SUBLANES
    gn = N_SSM_GROUPS * D_STATE

    @pl.when(pl.program_id(1) == 0)
    def _():
        ext_ref[0:halo, :] = jnp.zeros((halo, CONV_CH), F32)
        state_ref[...] = jnp.zeros_like(state_ref)

    ext_ref[halo:halo + CHUNK, 0:D_INNER] = x_ref[...].astype(F32)
    ext_ref[halo:halo + CHUNK, D_INNER:CONV_CH] = bc_ref[...].astype(F32)
    conv = cb_ref[...] + cw_ref[CONV_WIDTH - 1:CONV_WIDTH, :] * ext_ref[halo:halo + CHUNK, :]
    for j in range(CONV_WIDTH - 1):
        lo = halo - (CONV_WIDTH - 1) + j
        conv = conv + cw_ref[j:j + 1, :] * ext_ref[lo:lo + CHUNK, :]
    ext_ref[0:halo, :] = ext_ref[CHUNK:CHUNK + halo, :]
    act = _silu(conv)
    xs = act[:, 0:D_INNER]
    xs_b = xs.astype(BF16)
    bm = act[:, D_INNER:D_INNER + gn]
    cm = act[:, D_INNER + gn:CONV_CH]

    dt_in = dt_ref[...] + dtb_ref[...]
    dt = jnp.maximum(dt_in, 0.0) + jnp.log(1.0 + jnp.exp(-jnp.abs(dt_in)))
    da = dt * (-jnp.exp(alog_ref[...]))
    row = lax.broadcasted_iota(I32, (CHUNK, CHUNK), 0)
    col = lax.broadcasted_iota(I32, (CHUNK, CHUNK), 1)
    causal = row >= col
    tri = jnp.where(causal, 1.0, 0.0).astype(BF16)
    da_hi, da_lo = _split_bf16(da)
    a_cum = (jnp.dot(tri, da_hi, preferred_element_type=F32)
             + jnp.dot(tri, da_lo, preferred_element_type=F32))
    tr = jnp.concatenate([a_cum, dt], axis=1).T
    a_cum_t = tr[0:N_SSM_HEADS, :]
    dt_t = tr[N_SSM_HEADS:2 * N_SSM_HEADS, :]
    a_last_col = a_cum_t[:, CHUNK - 1:CHUNK]
    w_t = dt_t * jnp.exp(a_last_col - a_cum_t)
    dec = jnp.exp(a_cum[CHUNK - 1:CHUNK, :])
    dec8 = jnp.broadcast_to(dec, (V7X_SUBLANES, N_SSM_HEADS))
    dec_hi, dec_lo = _split_bf16(dec8)
    dec_exp = (jnp.dot(dec_hi, e2_ref[...], preferred_element_type=F32)
               + jnp.dot(dec_lo, e2_ref[...], preferred_element_type=F32))[0:1, :]

    lane = lax.broadcasted_iota(I32, (CHUNK, 2 * SSM_HEAD_DIM), 1)
    left = lane < SSM_HEAD_DIM
    for g in range(N_SSM_GROUPS):
        b_g = bm[:, g * D_STATE:(g + 1) * D_STATE]
        c_g = cm[:, g * D_STATE:(g + 1) * D_STATE]
        bt_g = b_g.T
        cb_g = lax.dot_general(c_g.astype(BF16), b_g.astype(BF16),
                               (((1,), (1,)), ((), ())), preferred_element_type=F32)
        for k in range(HEADS_PER_GROUP // 2):
            slab = slice((g * 4 + k) * 2 * SSM_HEAD_DIM, (g * 4 + k + 1) * 2 * SSM_HEAD_DIM)
            lhs_y = []
            lhs_s = []
            for sub in range(2):
                h = g * HEADS_PER_GROUP + 2 * k + sub
                colb = jnp.broadcast_to(a_cum[:, h:h + 1], (CHUNK, CHUNK))
                rowb = jnp.broadcast_to(a_cum_t[h:h + 1, :], (CHUNK, CHUNK))
                decay = jnp.exp(jnp.where(causal, colb - rowb, -jnp.inf))
                m_h = cb_g * decay * jnp.broadcast_to(dt_t[h:h + 1, :], (CHUNK, CHUNK))
                lhs_y.append(m_h.astype(BF16))
                lhs_y.append((c_g * jnp.exp(colb)).astype(BF16))
                lhs_s.append((bt_g * jnp.broadcast_to(w_t[h:h + 1, :], (CHUNK, CHUNK))).astype(BF16))
            x_slab = xs_b[:, slab]
            s_slab = state_ref[:, slab]
            s_slab_b = s_slab.astype(BF16)
            zero = jnp.zeros_like(x_slab)
            x_l = jnp.where(left, x_slab, zero)
            x_r = jnp.where(left, zero, x_slab)
            s_l = jnp.where(left, s_slab_b, zero)
            s_r = jnp.where(left, zero, s_slab_b)
            y_pair = jnp.dot(jnp.concatenate(lhs_y, axis=1),
                             jnp.concatenate([x_l, s_l, x_r, s_r], axis=0),
                             preferred_element_type=F32)
            y_ref[:, slab] = y_pair
            d_state = jnp.dot(jnp.concatenate(lhs_s, axis=1),
                              jnp.concatenate([x_l, x_r], axis=0),
                              preferred_element_type=F32)
            state_ref[:, slab] = s_slab * dec_exp[:, slab] + d_state

    gsz = D_INNER // N_SSM_GROUPS
    for g in range(N_SSM_GROUPS):
        cols = slice(g * gsz, (g + 1) * gsz)
        y = y_ref[:, cols] + dexp_ref[:, cols] * xs[:, cols]
        gated = y * _silu(z_ref[:, cols].astype(F32))
        var = jnp.mean(gated * gated, axis=-1, keepdims=True)
        o_ref[:, cols] = ((gated * lax.rsqrt(var + EPS)) * nw_ref[:, cols]).astype(o_ref.dtype)


def _ssd(zx, dt_raw, conv_w, conv_b, dt_bias, a_log, d_skip, norm_w, batch, seq_len):
    t = zx.shape[0]
    nc = seq_len // CHUNK
    d_exp = jnp.repeat(d_skip, SSM_HEAD_DIM).reshape(1, D_INNER)
    e2 = jnp.repeat(jnp.eye(N_SSM_HEADS, dtype=BF16), SSM_HEAD_DIM, axis=1)
    bc_col = (2 * D_INNER) // (2 * N_SSM_GROUPS * D_STATE)

    def rows(cb):
        return lambda b, c: (b * nc + c, cb)

    def whole(b, c):
        return (0, 0)

    return pl.pallas_call(
        _ssd_kernel,
        grid=(batch, nc),
        in_specs=[
            pl.BlockSpec((CHUNK, D_INNER), rows(0)),
            pl.BlockSpec((CHUNK, D_INNER), rows(1)),
            pl.BlockSpec((CHUNK, 2 * N_SSM_GROUPS * D_STATE), rows(bc_col)),
            pl.BlockSpec((CHUNK, N_SSM_HEADS), rows(0)),
            pl.BlockSpec((CONV_WIDTH, CONV_CH), whole),
            pl.BlockSpec((1, CONV_CH), whole),
            pl.BlockSpec((1, N_SSM_HEADS), whole),
            pl.BlockSpec((1, N_SSM_HEADS), whole),
            pl.BlockSpec((1, D_INNER), whole),
            pl.BlockSpec((1, D_INNER), whole),
            pl.BlockSpec((N_SSM_HEADS, D_INNER), whole),
        ],
        out_specs=pl.BlockSpec((CHUNK, D_INNER), rows(0)),
        out_shape=jax.ShapeDtypeStruct((t, D_INNER), BF16),
        scratch_shapes=[
            pltpu.VMEM((CHUNK + V7X_SUBLANES, CONV_CH), F32),
            pltpu.VMEM((D_STATE, D_INNER), F32),
            pltpu.VMEM((CHUNK, D_INNER), F32),
        ],
        compiler_params=_cparams(2, 48),
        name="ssd",
    )(zx, zx, zx, dt_raw, conv_w, conv_b.reshape(1, CONV_CH), dt_bias.reshape(1, N_SSM_HEADS),
      a_log.reshape(1, N_SSM_HEADS), d_exp, norm_w.reshape(1, D_INNER), e2)


def _route_kernel(h_ref, nw_ref, wt_ref, bias_ref, hn_ref, idx_ref, gw_ref, cnt_ref, carry_ref):
    tm = h_ref.shape[0]

    @pl.when(pl.program_id(0) == 0)
    def _():
        carry_ref[...] = jnp.zeros_like(carry_ref)

    h = h_ref[...]
    var = jnp.mean(h * h, axis=-1, keepdims=True)
    hn = (h * lax.rsqrt(var + EPS)) * nw_ref[...]
    hn_ref[...] = hn

    nt = (((1,), (1,)), ((), ()))
    hn_hi, hn_lo = _split_bf16(hn)
    w_hi, w_lo = _split_bf16(wt_ref[...])
    logits = (lax.dot_general(w_hi, hn_hi, nt, preferred_element_type=F32)
              + lax.dot_general(w_hi, hn_lo, nt, preferred_element_type=F32)
              + lax.dot_general(w_lo, hn_hi, nt, preferred_element_type=F32)
              + bias_ref[...])

    def lrow(r):
        return logits[r:r + 1, :]

    gl = [lrow(g) for g in range(N_EXPERT_GROUPS)]
    gmax = functools.reduce(jnp.maximum, gl)
    gidx = jnp.full((1, tm), N_EXPERT_GROUPS - 1, I32)
    for g in range(N_EXPERT_GROUPS - 2, -1, -1):
        gidx = jnp.where(gl[g] == gmax, g, gidx)
    gsum = functools.reduce(jnp.add, [jnp.exp(x - gmax) for x in gl])
    g_weight = 1.0 / gsum

    el = []
    for e in range(EXPERTS_PER_GROUP):
        v = lrow(EXPERT_ROW0 + e)
        for g in range(1, N_EXPERT_GROUPS):
            v = jnp.where(gidx == g, lrow(EXPERT_ROW0 + g * EXPERTS_PER_GROUP + e), v)
        el.append(v)

    def first_argmax(vals):
        m = functools.reduce(jnp.maximum, vals)
        idx = jnp.full((1, tm), len(vals) - 1, I32)
        for e in range(len(vals) - 2, -1, -1):
            idx = jnp.where(vals[e] == m, e, idx)
        return m, idx

    m1, i1 = first_argmax(el)
    el2 = [jnp.where(i1 == e, -jnp.inf, el[e]) for e in range(EXPERTS_PER_GROUP)]
    m2, i2 = first_argmax(el2)
    e21 = jnp.exp(m2 - m1)
    w1 = g_weight / (1.0 + e21)
    w2 = g_weight * e21 / (1.0 + e21)
    eid1 = gidx * EXPERTS_PER_GROUP + i1
    eid2 = gidx * EXPERTS_PER_GROUP + i2

    erow = lax.broadcasted_iota(I32, (N_EXPERTS, tm), 0)
    oh1 = erow == eid1
    oh2 = erow == eid2
    member = jnp.where(oh1 | oh2, 1.0, 0.0)
    s_i = lax.broadcasted_iota(I32, (tm, tm), 0)
    t_i = lax.broadcasted_iota(I32, (tm, tm), 1)
    before = jnp.where(s_i < t_i, 1.0, 0.0).astype(BF16)
    excl = jnp.dot(member.astype(BF16), before, preferred_element_type=F32)
    carry = carry_ref[...]
    tot = excl + carry[:, 0:1]
    rank1 = jnp.sum(jnp.where(oh1, tot, 0.0), axis=0, keepdims=True)
    rank2 = jnp.sum(jnp.where(oh2, tot, 0.0), axis=0, keepdims=True)
    carry = carry + jnp.sum(member, axis=1, keepdims=True)
    carry_ref[...] = carry
    cnt_ref[...] = carry.astype(I32)

    idx_ref[...] = jnp.zeros_like(idx_ref)
    idx_ref[0:1, :] = eid1
    idx_ref[1:2, :] = eid2
    idx_ref[2:3, :] = rank1.astype(I32)
    idx_ref[3:4, :] = rank2.astype(I32)
    gw_ref[...] = jnp.zeros_like(gw_ref)
    gw_ref[0:1, :] = w1
    gw_ref[1:2, :] = w2


def _route(h, nw, w_group, b_group, w_router, b_router):
    t, d = h.shape
    tm = min(ROUTE_TM, t)
    wt = jnp.zeros((ROUTE_ROWS, d), F32)
    wt = wt.at[0:N_EXPERT_GROUPS].set(w_group.T)
    wt = wt.at[EXPERT_ROW0:EXPERT_ROW0 + N_EXPERTS].set(
        jnp.transpose(w_router, (0, 2, 1)).reshape(N_EXPERTS, d))
    bias = jnp.zeros((ROUTE_ROWS, 1), F32)
    bias = bias.at[0:N_EXPERT_GROUPS, 0].set(b_group)
    bias = bias.at[EXPERT_ROW0:EXPERT_ROW0 + N_EXPERTS, 0].set(b_router.reshape(N_EXPERTS))
    return pl.pallas_call(
        _route_kernel,
        grid=(t // tm,),
        in_specs=[
            pl.BlockSpec((tm, d), lambda i: (i, 0)),
            pl.BlockSpec((1, d), lambda i: (0, 0)),
            pl.BlockSpec((ROUTE_ROWS, d), lambda i: (0, 0)),
            pl.BlockSpec((ROUTE_ROWS, 1), lambda i: (0, 0)),
        ],
        out_specs=[
            pl.BlockSpec((tm, d), lambda i: (i, 0)),
            pl.BlockSpec((V7X_SUBLANES, tm), lambda i: (0, i)),
            pl.BlockSpec((V7X_SUBLANES, tm), lambda i: (0, i)),
            pl.BlockSpec((N_EXPERTS, V7X_LANES), lambda i: (0, 0)),
        ],
        out_shape=[
            jax.ShapeDtypeStruct((t, d), F32),
            jax.ShapeDtypeStruct((V7X_SUBLANES, t), I32),
            jax.ShapeDtypeStruct((V7X_SUBLANES, t), F32),
            jax.ShapeDtypeStruct((N_EXPERTS, V7X_LANES), I32),
        ],
        scratch_shapes=[pltpu.VMEM((N_EXPERTS, V7X_LANES), F32)],
        compiler_params=_cparams(1, 40),
        name="moe_route",
    )(h, nw.reshape(1, d), wt, bias)


def _row_copy(src_hbm, src_row, dst_hbm, dst_row, sem):
    return pltpu.make_async_copy(src_hbm.at[pl.ds(src_row, 1)], dst_hbm.at[pl.ds(dst_row, 1)], sem)


def _dispatch_kernel(pos_ref, fill_start_ref, fill_n_ref, nv_ref, hn_hbm, xs_hbm, zeros_ref, sem,
                     tile_sem, *, n_tok, tm, n_tiles):
    step = pl.program_id(0)
    base = step * tm

    def tile_copy(i):
        return pltpu.make_async_copy(zeros_ref, xs_hbm.at[pl.ds(i * EXPERT_TM, EXPERT_TM)], tile_sem)

    def tile_start(i, carry):
        tile_copy(i).start()
        return carry

    def tile_wait(i, carry):
        tile_copy(i).wait()
        return carry

    @pl.when(step == 0)
    def _():
        zeros_ref[...] = jnp.zeros_like(zeros_ref)
        lax.fori_loop(nv_ref[0], n_tiles, tile_start, 0)

    def issue(i, carry):
        tok = base + i
        _row_copy(hn_hbm, tok, xs_hbm, pos_ref[tok], sem).start()
        _row_copy(hn_hbm, tok, xs_hbm, pos_ref[n_tok + tok], sem).start()
        return carry

    def drain(i, carry):
        _row_copy(hn_hbm, 0, xs_hbm, 0, sem).wait()
        return carry

    lax.fori_loop(0, tm, issue, 0)
    lax.fori_loop(0, 2 * tm, drain, 0)

    @pl.when(step == 0)
    def _():
        for e in range(N_EXPERTS):
            start = fill_start_ref[e]

            def fill(i, carry, start=start):
                _row_copy(hn_hbm, 0, xs_hbm, start + i, sem).start()
                return carry

            lax.fori_loop(0, fill_n_ref[e], fill, 0)
            lax.fori_loop(0, fill_n_ref[e], drain, 0)
        lax.fori_loop(nv_ref[0], n_tiles, tile_wait, 0)


def _dispatch(hn, pos, fill_start, fill_n, n_live, n_slots):
    t, d = hn.shape
    tm = min(DISPATCH_TM, t)
    return pl.pallas_call(
        functools.partial(_dispatch_kernel, n_tok=t, tm=tm, n_tiles=n_slots // EXPERT_TM),
        grid_spec=pltpu.PrefetchScalarGridSpec(
            num_scalar_prefetch=4,
            grid=(t // tm,),
            in_specs=[pl.BlockSpec(memory_space=pl.ANY)],
            out_specs=pl.BlockSpec(memory_space=pl.ANY),
            scratch_shapes=[pltpu.VMEM((EXPERT_TM, d), F32), pltpu.SemaphoreType.DMA(()),
                            pltpu.SemaphoreType.DMA(())],
        ),
        out_shape=jax.ShapeDtypeStruct((n_slots, d), F32),
        compiler_params=pltpu.CompilerParams(dimension_semantics=("arbitrary",),
                                             has_side_effects=True),
        name="moe_dispatch",
    )(pos, fill_start, fill_n, n_live, hn)


def _expert_kernel(te_ref, nv_ref, xs_ref, wg_ref, wu_ref, wd_ref, ys_ref, wg_b, wu_b, wd_b):
    i = pl.program_id(0)
    live = i < nv_ref[0]
    changed = jnp.logical_or(i == 0, te_ref[i] != te_ref[jnp.maximum(i - 1, 0)])

    @pl.when(jnp.logical_and(live, changed))
    def _():
        wg_b[...] = wg_ref[0].astype(BF16)
        wu_b[...] = wu_ref[0].astype(BF16)
        wd_b[...] = wd_ref[0].astype(BF16)

    @pl.when(live)
    def _():
        x = xs_ref[...].astype(BF16)
        gate = jnp.dot(x, wg_b[...], preferred_element_type=F32)
        up = jnp.dot(x, wu_b[...], preferred_element_type=F32)
        act = (_silu(gate) * up).astype(BF16)
        ys_ref[...] = jnp.dot(act, wd_b[...], preferred_element_type=F32)

    @pl.when(jnp.logical_not(live))
    def _():
        ys_ref[...] = jnp.zeros_like(ys_ref)


def _experts(xs, tile_expert, n_live, w_gate, w_up, w_down):
    n_slots, d = xs.shape
    n_tiles = n_slots // EXPERT_TM
    f = w_gate.shape[2]

    def rows(i, te, nv):
        return (jnp.minimum(i, nv[0] - 1), 0)

    def out_rows(i, te, nv):
        return (i, 0)

    def wsel(i, te, nv):
        return (te[i], 0, 0)

    return pl.pallas_call(
        _expert_kernel,
        grid_spec=pltpu.PrefetchScalarGridSpec(
            num_scalar_prefetch=2,
            grid=(n_tiles,),
            in_specs=[
                pl.BlockSpec((EXPERT_TM, d), rows),
                pl.BlockSpec((1, d, f), wsel),
                pl.BlockSpec((1, d, f), wsel),
                pl.BlockSpec((1, f, d), wsel),
            ],
            out_specs=pl.BlockSpec((EXPERT_TM, d), out_rows),
            scratch_shapes=[pltpu.VMEM((d, f), BF16), pltpu.VMEM((d, f), BF16),
                            pltpu.VMEM((f, d), BF16)],
        ),
        out_shape=jax.ShapeDtypeStruct((n_slots, d), F32),
        compiler_params=_cparams(1, 52),
        name="moe_experts",
    )(tile_expert, n_live, xs, w_gate, w_up, w_down)


def _combine_kernel(pos_ref, h_ref, w1_ref, w2_ref, fw_ref, ys_hbm, o_ref, ybuf, sem,
                    *, n_tok, final_norm):
    tm = h_ref.shape[0]
    base = pl.program_id(0) * tm

    def row_dma(slot, i, src_row):
        return pltpu.make_async_copy(ys_hbm.at[pl.ds(src_row, 1)], ybuf.at[slot, pl.ds(i, 1)], sem)

    def issue(i, carry):
        row_dma(0, i, pos_ref[base + i]).start()
        row_dma(1, i, pos_ref[n_tok + base + i]).start()
        return carry

    def drain(i, carry):
        row_dma(0, 0, 0).wait()
        return carry

    lax.fori_loop(0, tm, issue, 0)
    lax.fori_loop(0, 2 * tm, drain, 0)
    out = h_ref[...] + w1_ref[...] * ybuf[0] + w2_ref[...] * ybuf[1]
    if final_norm:
        var = jnp.mean(out * out, axis=-1, keepdims=True)
        out = (out * lax.rsqrt(var + EPS)) * fw_ref[...]
    o_ref[...] = out


def _combine(h, ys, pos, w1, w2, final_w, final_norm):
    t, d = h.shape
    tm = min(COMBINE_TM, t)
    return pl.pallas_call(
        functools.partial(_combine_kernel, n_tok=t, final_norm=final_norm),
        grid_spec=pltpu.PrefetchScalarGridSpec(
            num_scalar_prefetch=1,
            grid=(t // tm,),
            in_specs=[
                pl.BlockSpec((tm, d), lambda i, p: (i, 0)),
                pl.BlockSpec((tm, 1), lambda i, p: (i, 0)),
                pl.BlockSpec((tm, 1), lambda i, p: (i, 0)),
                pl.BlockSpec((1, d), lambda i, p: (0, 0)),
                pl.BlockSpec(memory_space=pl.ANY),
            ],
            out_specs=pl.BlockSpec((tm, d), lambda i, p: (i, 0)),
            scratch_shapes=[pltpu.VMEM((2, tm, d), F32), pltpu.SemaphoreType.DMA(())],
        ),
        out_shape=jax.ShapeDtypeStruct((t, d), F32),
        compiler_params=_cparams(1, 32),
        name="moe_combine",
    )(pos, h, w1.reshape(t, 1), w2.reshape(t, 1), final_w.reshape(1, d), ys)


def _moe(h, nw, w_group, b_group, w_router, b_router, w_gate, w_up, w_down, final_w, final_norm):
    t, d = h.shape
    hn, idx, gw, cnt = _route(h, nw, w_group, b_group, w_router, b_router)
    counts = cnt[:, 0]
    padded = ((counts + EXPERT_TM - 1) // EXPERT_TM) * EXPERT_TM
    seg_end = jnp.cumsum(padded)
    seg_start = seg_end - padded
    pos = jnp.concatenate([seg_start[idx[0]] + idx[2], seg_start[idx[1]] + idx[3]]).astype(I32)
    n_tiles = (2 * t) // EXPERT_TM + N_EXPERTS
    n_live = (seg_end[-1] // EXPERT_TM).astype(I32)
    tile_first = jnp.minimum(jnp.arange(n_tiles, dtype=I32), n_live - 1) * EXPERT_TM
    tile_expert = jnp.sum(tile_first[:, None] >= seg_end[None, :], axis=1).astype(I32)
    n_live = n_live.reshape(1)
    xs = _dispatch(hn, pos, (seg_start + counts).astype(I32), (padded - counts).astype(I32),
                   n_live, n_tiles * EXPERT_TM)
    ys = _experts(xs, tile_expert, n_live, w_gate, w_up, w_down)
    return _combine(h, ys, pos, gw[0], gw[1], final_w, final_norm)


def kernel(x, norm_mix_w, norm_ffn_w, final_norm_w, mix_w_in, pool_w, pool_scale, attn_sinks,
           mix_w_out, ssm_w_in, ssm_conv_w, ssm_conv_b, ssm_dt_bias, ssm_a_log, ssm_d,
           ssm_norm_w, ssm_w_out, moe_w_group, moe_b_group, moe_w_router, moe_b_router,
           moe_w_gate, moe_w_up, moe_w_down):
    batch, seq_len, d = x.shape
    h = x.reshape(batch * seq_len, d)

    proj = _norm_matmul(h, norm_mix_w[0], mix_w_in[0], MIX_IN_WIDTH, PROJ_TN, BF16)
    mixed = _pool_attn(proj, pool_w[0], pool_scale[0], attn_sinks[0], seq_len)
    h = _matmul_residual(mixed, mix_w_out[0], h, PROJ_TN)
    h = _moe(h, norm_ffn_w[0], moe_w_group[0], moe_b_group[0], moe_w_router[0], moe_b_router[0],
             moe_w_gate[0], moe_w_up[0], moe_w_down[0], final_norm_w, False)

    zx = _norm_matmul(h, norm_mix_w[1], ssm_w_in[0], ZX_WIDTH, PROJ_TN, BF16)
    dt_raw = _norm_matmul(h, norm_mix_w[1], ssm_w_in[0][:, ZX_WIDTH:], N_SSM_HEADS, N_SSM_HEADS, F32)
    y = _ssd(zx, dt_raw, ssm_conv_w[0], ssm_conv_b[0], ssm_dt_bias[0], ssm_a_log[0], ssm_d[0],
             ssm_norm_w[0], batch, seq_len)
    h = _matmul_residual(y, ssm_w_out[0], h, PROJ_TN // 2)
    h = _moe(h, norm_ffn_w[1], moe_w_group[1], moe_b_group[1], moe_w_router[1], moe_b_router[1],
             moe_w_gate[1], moe_w_up[1], moe_w_down[1], final_norm_w, True)
    return h.reshape(batch, seq_len, d)
```

```python
import functools
import math

import jax
import jax.numpy as jnp
import numpy as np
from jax import lax
from jax.experimental import pallas as pl
from jax.experimental.pallas import tpu as pltpu

F32 = jnp.float32
BF16 = jnp.bfloat16
I32 = jnp.int32

D_MODEL = 2048
EPS = 1e-6
POOL_WINDOWS = (2, 4, 8, 16)
POOL_WIDTH = 1024
POOL_GROUP = 256
HEAD_DIM = 64
N_Q_HEADS = 16
N_KV_HEADS = 4
Q_PER_KV = 4
ATTN_WIDTH = 1024
KV_WIDTH = 256
WINDOW = 128
BLOCK = 128
MIX_IN_WIDTH = 2560
D_INNER = 4096
SSM_HEAD_DIM = 64
N_SSM_HEADS = 64
N_SSM_GROUPS = 8
HEADS_PER_GROUP = 8
D_STATE = 128
CONV_WIDTH = 4
CHUNK = 128
CONV_CH = D_INNER + 2 * N_SSM_GROUPS * D_STATE
ZX_WIDTH = D_INNER + CONV_CH
N_EXPERT_GROUPS = 4
EXPERTS_PER_GROUP = 4
N_EXPERTS = 16
D_FF_EXPERT = 512
SLOPES = tuple(float(2.0 ** (-8.0 * (i + 1) / N_Q_HEADS)) for i in range(N_Q_HEADS))

V7X_LANES = 128
V7X_SUBLANES = 8
V7X_VMEM_BYTES = 64 * 1024 * 1024

PROJ_TM = 1024
PROJ_TN = 512
NORM_ROWS = 128
ROUTE_TM = 512
EXPERT_TM = 256
COMBINE_TM = 256
ROUTE_ROWS = 32
EXPERT_ROW0 = 8


def _cparams(n_axes, vmem_mb):
    return pltpu.CompilerParams(
        dimension_semantics=("arbitrary",) * n_axes,
        vmem_limit_bytes=vmem_mb * 1024 * 1024,
    )


def _silu(x):
    return x / (1.0 + jnp.exp(-x))


def _split_bf16(x):
    hi = x.astype(BF16)
    lo = (x - hi.astype(F32)).astype(BF16)
    return hi, lo


def _norm_matmul_kernel(x_ref, nw_ref, w_ref, o_ref, xn_ref):
    tm = x_ref.shape[0]

    @pl.when(pl.program_id(1) == 0)
    def _():
        def body(r, carry):
            rows = pl.ds(pl.multiple_of(r * NORM_ROWS, NORM_ROWS), NORM_ROWS)
            x = x_ref[rows, :]
            var = jnp.mean(x * x, axis=-1, keepdims=True)
            xn_ref[rows, :] = ((x * lax.rsqrt(var + EPS)) * nw_ref[...]).astype(BF16)
            return carry

        lax.fori_loop(0, tm // NORM_ROWS, body, 0)

    o_ref[...] = jnp.dot(xn_ref[...], w_ref[...].astype(BF16),
                         preferred_element_type=F32).astype(o_ref.dtype)


def _norm_matmul(x, nw, w, n_out, tn, out_dtype):
    t, d = x.shape
    tm = min(PROJ_TM, t)
    return pl.pallas_call(
        _norm_matmul_kernel,
        grid=(t // tm, n_out // tn),
        in_specs=[
            pl.BlockSpec((tm, d), lambda i, j: (i, 0)),
            pl.BlockSpec((1, d), lambda i, j: (0, 0)),
            pl.BlockSpec((d, tn), lambda i, j: (0, j)),
        ],
        out_specs=pl.BlockSpec((tm, tn), lambda i, j: (i, j)),
        out_shape=jax.ShapeDtypeStruct((t, n_out), out_dtype),
        scratch_shapes=[pltpu.VMEM((tm, d), BF16)],
        compiler_params=_cparams(2, 48),
        name="norm_matmul",
    )(x, nw.reshape(1, d), w)


def _matmul_res_kernel(a_ref, w_ref, r_ref, o_ref):
    o_ref[...] = r_ref[...] + jnp.dot(a_ref[...], w_ref[...].astype(BF16),
                                      preferred_element_type=F32)


def _matmul_residual(a, w, res, tn):
    t, k = a.shape
    n = w.shape[1]
    tm = min(PROJ_TM, t)
    return pl.pallas_call(
        _matmul_res_kernel,
        grid=(t // tm, n // tn),
        in_specs=[
            pl.BlockSpec((tm, k), lambda i, j: (i, 0)),
            pl.BlockSpec((k, tn), lambda i, j: (0, j)),
            pl.BlockSpec((tm, tn), lambda i, j: (i, j)),
        ],
        out_specs=pl.BlockSpec((tm, tn), lambda i, j: (i, j)),
        out_shape=jax.ShapeDtypeStruct((t, n), F32),
        compiler_params=_cparams(2, 48),
        name="matmul_residual",
    )(a, w, res)


def _pool_attn_kernel(sinks_ref, u_ref, up_ref, q_ref, k_ref, kp_ref, v_ref, vp_ref,
                      pw_ref, ps_ref, o_ref, *, blocks_per_seq):
    blk = pl.program_id(0) % blocks_per_seq
    first = blk == 0
    row = lax.broadcasted_iota(I32, (BLOCK, 2 * BLOCK), 0)
    col = lax.broadcasted_iota(I32, (BLOCK, 2 * BLOCK), 1)
    dist = row + BLOCK - col

    u_cur = u_ref[...]
    u_prev = jnp.where(first, jnp.zeros_like(u_cur), up_ref[...])
    u_ext = jnp.concatenate([u_prev, u_cur], axis=0)
    pos = blk * BLOCK + lax.broadcasted_iota(I32, (BLOCK, 1), 0)
    for g, w in enumerate(POOL_WINDOWS):
        cols = slice(g * POOL_GROUP, (g + 1) * POOL_GROUP)
        band = jnp.where((dist >= 0) & (dist < w), 1.0, 0.0).astype(BF16)
        wsum = jnp.dot(band, u_ext[:, cols], preferred_element_type=F32)
        count = jnp.minimum(pos + 1, w).astype(F32)
        pooled = (wsum / count - u_cur[:, cols].astype(F32)).astype(BF16)
        out_g = jnp.dot(pooled, pw_ref[g].astype(BF16), preferred_element_type=F32)
        o_ref[:, cols] = (out_g * ps_ref[:, cols]).astype(o_ref.dtype)

    kk = jnp.concatenate([kp_ref[...], k_ref[...]], axis=0)
    vv = jnp.concatenate([vp_ref[...], v_ref[...]], axis=0)
    valid = (dist >= 0) & (dist < WINDOW) & ((col >= BLOCK) | jnp.logical_not(first))
    dist_f = dist.astype(F32)
    scale = HEAD_DIM ** -0.5
    for pair in range(N_Q_HEADS // 2):
        q_pair = q_ref[:, pair * 2 * HEAD_DIM:(pair + 1) * 2 * HEAD_DIM]
        outs = []
        for sub in range(2):
            h = pair * 2 + sub
            g = h // Q_PER_KV
            qh = q_pair[:, sub * HEAD_DIM:(sub + 1) * HEAD_DIM]
            kg = kk[:, g * HEAD_DIM:(g + 1) * HEAD_DIM]
            vg = vv[:, g * HEAD_DIM:(g + 1) * HEAD_DIM]
            s = lax.dot_general(qh, kg, (((1,), (1,)), ((), ())),
                                preferred_element_type=F32) * scale
            s = jnp.where(valid, s - SLOPES[h] * dist_f, -jnp.inf)
            sink = sinks_ref[h]
            m = jnp.maximum(jnp.max(s, axis=-1, keepdims=True), sink)
            p = jnp.exp(s - m)
            denom = jnp.sum(p, axis=-1, keepdims=True) + jnp.exp(sink - m)
            probs = (p / denom).astype(BF16)
            outs.append(jnp.dot(probs, vg, preferred_element_type=F32))
        lo = ATTN_WIDTH + pair * 2 * HEAD_DIM
        o_ref[:, lo:lo + 2 * HEAD_DIM] = jnp.concatenate(outs, axis=1).astype(o_ref.dtype)


def _pool_attn(proj, pool_w, pool_scale, sinks, seq_len):
    t = proj.shape[0]
    nblk = t // BLOCK
    kcol = (POOL_WIDTH + ATTN_WIDTH) // KV_WIDTH

    def cur(cb):
        return lambda i: (i, cb)

    def prev(cb):
        return lambda i: (jnp.maximum(i - 1, 0), cb)

    return pl.pallas_call(
        functools.partial(_pool_attn_kernel, blocks_per_seq=seq_len // BLOCK),
        grid=(nblk,),
        in_specs=[
            pl.BlockSpec(memory_space=pltpu.SMEM),
            pl.BlockSpec((BLOCK, POOL_WIDTH), cur(0)),
            pl.BlockSpec((BLOCK, POOL_WIDTH), prev(0)),
            pl.BlockSpec((BLOCK, ATTN_WIDTH), cur(1)),
            pl.BlockSpec((BLOCK, KV_WIDTH), cur(kcol)),
            pl.BlockSpec((BLOCK, KV_WIDTH), prev(kcol)),
            pl.BlockSpec((BLOCK, KV_WIDTH), cur(kcol + 1)),
            pl.BlockSpec((BLOCK, KV_WIDTH), prev(kcol + 1)),
            pl.BlockSpec((len(POOL_WINDOWS), POOL_GROUP, POOL_GROUP), lambda i: (0, 0, 0)),
            pl.BlockSpec((1, POOL_WIDTH), lambda i: (0, 0)),
        ],
        out_specs=pl.BlockSpec((BLOCK, POOL_WIDTH + ATTN_WIDTH), lambda i: (i, 0)),
        out_shape=jax.ShapeDtypeStruct((t, POOL_WIDTH + ATTN_WIDTH), BF16),
        compiler_params=_cparams(1, 32),
        name="pool_attn",
    )(sinks, proj, proj, proj, proj, proj, proj, proj, pool_w, pool_scale.reshape(1, POOL_WIDTH))


def _ssd_kernel(z_ref, x_ref, bc_ref, dt_ref, cw_ref, cb_ref, dtb_ref, alog_ref, dexp_ref,
                nw_ref, e2_ref, o_ref, ext_ref, state_ref, y_ref):
    halo = V7X_SUBLANES
    gn = N_SSM_GROUPS * D_STATE

    @pl.when(pl.program_id(1) == 0)
    def _():
        ext_ref[0:halo, :] = jnp.zeros((halo, CONV_CH), F32)
        state_ref[...] = jnp.zeros_like(state_ref)

    ext_ref[halo:halo + CHUNK, 0:D_INNER] = x_ref[...].astype(F32)
    ext_ref[halo:halo + CHUNK, D_INNER:CONV_CH] = bc_ref[...].astype(F32)
    conv = cb_ref[...] + cw_ref[CONV_WIDTH - 1:CONV_WIDTH, :] * ext_ref[halo:halo + CHUNK, :]
    for j in range(CONV_WIDTH - 1):
        lo = halo - (CONV_WIDTH - 1) + j
        conv = conv + cw_ref[j:j + 1, :] * ext_ref[lo:lo + CHUNK, :]
    ext_ref[0:halo, :] = ext_ref[CHUNK:CHUNK + halo, :]
    act = _silu(conv)
    xs = act[:, 0:D_INNER]
    xs_b = xs.astype(BF16)
    bm = act[:, D_INNER:D_INNER + gn]
    cm = act[:, D_INNER + gn:CONV_CH]

    dt_in = dt_ref[...] + dtb_ref[...]
    dt = jnp.maximum(dt_in, 0.0) + jnp.log(1.0 + jnp.exp(-jnp.abs(dt_in)))
    da = dt * (-jnp.exp(alog_ref[...]))
    row = lax.broadcasted_iota(I32, (CHUNK, CHUNK), 0)
    col = lax.broadcasted_iota(I32, (CHUNK, CHUNK), 1)
    causal = row >= col
    tri = jnp.where(causal, 1.0, 0.0).astype(BF16)
    da_hi, da_lo = _split_bf16(da)
    a_cum = (jnp.dot(tri, da_hi, preferred_element_type=F32)
             + jnp.dot(tri, da_lo, preferred_element_type=F32))
    tr = jnp.concatenate([a_cum, dt], axis=1).T
    a_cum_t = tr[0:N_SSM_HEADS, :]
    dt_t = tr[N_SSM_HEADS:2 * N_SSM_HEADS, :]
    a_last_col = a_cum_t[:, CHUNK - 1:CHUNK]
    w_t = dt_t * jnp.exp(a_last_col - a_cum_t)
    dec = jnp.exp(a_cum[CHUNK - 1:CHUNK, :])
    dec8 = jnp.broadcast_to(dec, (V7X_SUBLANES, N_SSM_HEADS))
    dec_hi, dec_lo = _split_bf16(dec8)
    dec_exp = (jnp.dot(dec_hi, e2_ref[...], preferred_element_type=F32)
               + jnp.dot(dec_lo, e2_ref[...], preferred_element_type=F32))[0:1, :]

    lane = lax.broadcasted_iota(I32, (CHUNK, 2 * SSM_HEAD_DIM), 1)
    left = lane < SSM_HEAD_DIM
    for g in range(N_SSM_GROUPS):
        b_g = bm[:, g * D_STATE:(g + 1) * D_STATE]
        c_g = cm[:, g * D_STATE:(g + 1) * D_STATE]
        bt_g = b_g.T
        cb_g = lax.dot_general(c_g.astype(BF16), b_g.astype(BF16),
                               (((1,), (1,)), ((), ())), preferred_element_type=F32)
        for k in range(HEADS_PER_GROUP // 2):
            slab = slice((g * 4 + k) * 2 * SSM_HEAD_DIM, (g * 4 + k + 1) * 2 * SSM_HEAD_DIM)
            lhs_y = []
            lhs_s = []
            for sub in range(2):
                h = g * HEADS_PER_GROUP + 2 * k + sub
                colb = jnp.broadcast_to(a_cum[:, h:h + 1], (CHUNK, CHUNK))
                rowb = jnp.broadcast_to(a_cum_t[h:h + 1, :], (CHUNK, CHUNK))
                decay = jnp.exp(jnp.where(causal, colb - rowb, -jnp.inf))
                m_h = cb_g * decay * jnp.broadcast_to(dt_t[h:h + 1, :], (CHUNK, CHUNK))
                lhs_y.append(m_h.astype(BF16))
                lhs_y.append((c_g * jnp.exp(colb)).astype(BF16))
                lhs_s.append((bt_g * jnp.broadcast_to(w_t[h:h + 1, :], (CHUNK, CHUNK))).astype(BF16))
            x_slab = xs_b[:, slab]
            s_slab = state_ref[:, slab]
            s_slab_b = s_slab.astype(BF16)
            zero = jnp.zeros_like(x_slab)
            x_l = jnp.where(left, x_slab, zero)
            x_r = jnp.where(left, zero, x_slab)
            s_l = jnp.where(left, s_slab_b, zero)
            s_r = jnp.where(left, zero, s_slab_b)
            y_pair = jnp.dot(jnp.concatenate(lhs_y, axis=1),
                             jnp.concatenate([x_l, s_l, x_r, s_r], axis=0),
                             preferred_element_type=F32)
            y_ref[:, slab] = y_pair
            d_state = jnp.dot(jnp.concatenate(lhs_s, axis=1),
                              jnp.concatenate([x_l, x_r], axis=0),
                              preferred_element_type=F32)
            state_ref[:, slab] = s_slab * dec_exp[:, slab] + d_state

    gsz = D_INNER // N_SSM_GROUPS
    for g in range(N_SSM_GROUPS):
        cols = slice(g * gsz, (g + 1) * gsz)
        y = y_ref[:, cols] + dexp_ref[:, cols] * xs[:, cols]
        gated = y * _silu(z_ref[:, cols].astype(F32))
        var = jnp.mean(gated * gated, axis=-1, keepdims=True)
        o_ref[:, cols] = ((gated * lax.rsqrt(var + EPS)) * nw_ref[:, cols]).astype(o_ref.dtype)


def _ssd(zx, dt_raw, conv_w, conv_b, dt_bias, a_log, d_skip, norm_w, batch, seq_len):
    t = zx.shape[0]
    nc = seq_len // CHUNK
    d_exp = jnp.repeat(d_skip, SSM_HEAD_DIM).reshape(1, D_INNER)
    e2 = jnp.repeat(jnp.eye(N_SSM_HEADS, dtype=BF16), SSM_HEAD_DIM, axis=1)
    bc_col = (2 * D_INNER) // (2 * N_SSM_GROUPS * D_STATE)

    def rows(cb):
        return lambda b, c: (b * nc + c, cb)

    def whole(b, c):
        return (0, 0)

    return pl.pallas_call(
        _ssd_kernel,
        grid=(batch, nc),
        in_specs=[
            pl.BlockSpec((CHUNK, D_INNER), rows(0)),
            pl.BlockSpec((CHUNK, D_INNER), rows(1)),
            pl.BlockSpec((CHUNK, 2 * N_SSM_GROUPS * D_STATE), rows(bc_col)),
            pl.BlockSpec((CHUNK, N_SSM_HEADS), rows(0)),
            pl.BlockSpec((CONV_WIDTH, CONV_CH), whole),
            pl.BlockSpec((1, CONV_CH), whole),
            pl.BlockSpec((1, N_SSM_HEADS), whole),
            pl.BlockSpec((1, N_SSM_HEADS), whole),
            pl.BlockSpec((1, D_INNER), whole),
            pl.BlockSpec((1, D_INNER), whole),
            pl.BlockSpec((N_SSM_HEADS, D_INNER), whole),
        ],
        out_specs=pl.BlockSpec((CHUNK, D_INNER), rows(0)),
        out_shape=jax.ShapeDtypeStruct((t, D_INNER), BF16),
        scratch_shapes=[
            pltpu.VMEM((CHUNK + V7X_SUBLANES, CONV_CH), F32),
            pltpu.VMEM((D_STATE, D_INNER), F32),
            pltpu.VMEM((CHUNK, D_INNER), F32),
        ],
        compiler_params=_cparams(2, 48),
        name="ssd",
    )(zx, zx, zx, dt_raw, conv_w, conv_b.reshape(1, CONV_CH), dt_bias.reshape(1, N_SSM_HEADS),
      a_log.reshape(1, N_SSM_HEADS), d_exp, norm_w.reshape(1, D_INNER), e2)


def _route_kernel(h_ref, nw_ref, wt_ref, bias_ref, hn_ref, idx_ref, gw_ref, cnt_ref, carry_ref):
    tm = h_ref.shape[0]

    @pl.when(pl.program_id(0) == 0)
    def _():
        carry_ref[...] = jnp.zeros_like(carry_ref)

    h = h_ref[...]
    var = jnp.mean(h * h, axis=-1, keepdims=True)
    hn = (h * lax.rsqrt(var + EPS)) * nw_ref[...]
    hn_ref[...] = hn

    nt = (((1,), (1,)), ((), ()))
    hn_hi, hn_lo = _split_bf16(hn)
    w_hi, w_lo = _split_bf16(wt_ref[...])
    logits = (lax.dot_general(w_hi, hn_hi, nt, preferred_element_type=F32)
              + lax.dot_general(w_hi, hn_lo, nt, preferred_element_type=F32)
              + lax.dot_general(w_lo, hn_hi, nt, preferred_element_type=F32)
              + bias_ref[...])

    def lrow(r):
        return logits[r:r + 1, :]

    gl = [lrow(g) for g in range(N_EXPERT_GROUPS)]
    gmax = functools.reduce(jnp.maximum, gl)
    gidx = jnp.full((1, tm), N_EXPERT_GROUPS - 1, I32)
    for g in range(N_EXPERT_GROUPS - 2, -1, -1):
        gidx = jnp.where(gl[g] == gmax, g, gidx)
    gsum = functools.reduce(jnp.add, [jnp.exp(x - gmax) for x in gl])
    g_weight = 1.0 / gsum

    el = []
    for e in range(EXPERTS_PER_GROUP):
        v = lrow(EXPERT_ROW0 + e)
        for g in range(1, N_EXPERT_GROUPS):
            v = jnp.where(gidx == g, lrow(EXPERT_ROW0 + g * EXPERTS_PER_GROUP + e), v)
        el.append(v)

    def first_argmax(vals):
        m = functools.reduce(jnp.maximum, vals)
        idx = jnp.full((1, tm), len(vals) - 1, I32)
        for e in range(len(vals) - 2, -1, -1):
            idx = jnp.where(vals[e] == m, e, idx)
        return m, idx

    m1, i1 = first_argmax(el)
    el2 = [jnp.where(i1 == e, -jnp.inf, el[e]) for e in range(EXPERTS_PER_GROUP)]
    m2, i2 = first_argmax(el2)
    e21 = jnp.exp(m2 - m1)
    w1 = g_weight / (1.0 + e21)
    w2 = g_weight * e21 / (1.0 + e21)
    eid1 = gidx * EXPERTS_PER_GROUP + i1
    eid2 = gidx * EXPERTS_PER_GROUP + i2

    erow = lax.broadcasted_iota(I32, (N_EXPERTS, tm), 0)
    oh1 = erow == eid1
    oh2 = erow == eid2
    member = jnp.where(oh1 | oh2, 1.0, 0.0)
    s_i = lax.broadcasted_iota(I32, (tm, tm), 0)
    t_i = lax.broadcasted_iota(I32, (tm, tm), 1)
    before = jnp.where(s_i < t_i, 1.0, 0.0).astype(BF16)
    excl = jnp.dot(member.astype(BF16), before, preferred_element_type=F32)
    carry = carry_ref[...]
    tot = excl + carry[:, 0:1]
    rank1 = jnp.sum(jnp.where(oh1, tot, 0.0), axis=0, keepdims=True)
    rank2 = jnp.sum(jnp.where(oh2, tot, 0.0), axis=0, keepdims=True)
    carry = carry + jnp.sum(member, axis=1, keepdims=True)
    carry_ref[...] = carry
    cnt_ref[...] = carry.astype(I32)

    idx_ref[...] = jnp.zeros_like(idx_ref)
    idx_ref[0:1, :] = eid1
    idx_ref[1:2, :] = eid2
    idx_ref[2:3, :] = rank1.astype(I32)
    idx_ref[3:4, :] = rank2.astype(I32)
    gw_ref[...] = jnp.zeros_like(gw_ref)
    gw_ref[0:1, :] = w1
    gw_ref[1:2, :] = w2


def _route(h, nw, w_group, b_group, w_router, b_router):
    t, d = h.shape
    tm = min(ROUTE_TM, t)
    wt = jnp.zeros((ROUTE_ROWS, d), F32)
    wt = wt.at[0:N_EXPERT_GROUPS].set(w_group.T)
    wt = wt.at[EXPERT_ROW0:EXPERT_ROW0 + N_EXPERTS].set(
        jnp.transpose(w_router, (0, 2, 1)).reshape(N_EXPERTS, d))
    bias = jnp.zeros((ROUTE_ROWS, 1), F32)
    bias = bias.at[0:N_EXPERT_GROUPS, 0].set(b_group)
    bias = bias.at[EXPERT_ROW0:EXPERT_ROW0 + N_EXPERTS, 0].set(b_router.reshape(N_EXPERTS))
    return pl.pallas_call(
        _route_kernel,
        grid=(t // tm,),
        in_specs=[
            pl.BlockSpec((tm, d), lambda i: (i, 0)),
            pl.BlockSpec((1, d), lambda i: (0, 0)),
            pl.BlockSpec((ROUTE_ROWS, d), lambda i: (0, 0)),
            pl.BlockSpec((ROUTE_ROWS, 1), lambda i: (0, 0)),
        ],
        out_specs=[
            pl.BlockSpec((tm, d), lambda i: (i, 0)),
            pl.BlockSpec((V7X_SUBLANES, tm), lambda i: (0, i)),
            pl.BlockSpec((V7X_SUBLANES, tm), lambda i: (0, i)),
            pl.BlockSpec((N_EXPERTS, V7X_LANES), lambda i: (0, 0)),
        ],
        out_shape=[
            jax.ShapeDtypeStruct((t, d), F32),
            jax.ShapeDtypeStruct((V7X_SUBLANES, t), I32),
            jax.ShapeDtypeStruct((V7X_SUBLANES, t), F32),
            jax.ShapeDtypeStruct((N_EXPERTS, V7X_LANES), I32),
        ],
        scratch_shapes=[pltpu.VMEM((N_EXPERTS, V7X_LANES), F32)],
        compiler_params=_cparams(1, 40),
        name="moe_route",
    )(h, nw.reshape(1, d), wt, bias)


def _expert_kernel(src_ref, te_ref, nv_ref, hn_hbm, wg_ref, wu_ref, wd_ref, ys_ref,
                   xbuf, wg_b, wu_b, wd_b, sems):
    i = pl.program_id(0)
    n_live = nv_ref[0]
    live = i < n_live
    tm = xbuf.shape[1]

    def row_copy(tile_slot, r, src_row):
        return pltpu.make_async_copy(hn_hbm.at[pl.ds(src_row, 1)],
                                     xbuf.at[tile_slot, pl.ds(r, 1)], sems.at[tile_slot])

    def gather_start(tile):
        tile_slot = tile % 2
        base = tile * tm

        def body(r, carry):
            row_copy(tile_slot, r, src_ref[base + r]).start()
            return carry

        lax.fori_loop(0, tm, body, 0, unroll=8)

    def gather_wait(tile_slot):
        def body(r, carry):
            row_copy(tile_slot, 0, 0).wait()
            return carry

        lax.fori_loop(0, tm, body, 0, unroll=8)

    @pl.when(i == 0)
    def _():
        gather_start(i)

    @pl.when(i + 1 < n_live)
    def _():
        gather_start(i + 1)

    changed = jnp.logical_or(i == 0, te_ref[i] != te_ref[jnp.maximum(i - 1, 0)])

    @pl.when(jnp.logical_and(live, changed))
    def _():
        wg_b[...] = wg_ref[0, 0].astype(BF16)
        wu_b[...] = wu_ref[0, 0].astype(BF16)
        wd_b[...] = wd_ref[0, 0].astype(BF16)

    @pl.when(live)
    def _():
        gather_wait(i % 2)
        x = xbuf[i % 2].astype(BF16)
        gate = jnp.dot(x, wg_b[...], preferred_element_type=F32)
        up = jnp.dot(x, wu_b[...], preferred_element_type=F32)
        act = (_silu(gate) * up).astype(BF16)
        ys_ref[...] = jnp.dot(act, wd_b[...], preferred_element_type=F32)

    @pl.when(jnp.logical_not(live))
    def _():
        ys_ref[...] = jnp.zeros_like(ys_ref)


def _experts(hn, src, tile_expert, n_live, w_gate, w_up, w_down, layer):
    d = hn.shape[1]
    n_slots = src.shape[0]
    f = w_gate.shape[3]

    def wsel(i, src_r, te, nv):
        return (layer, te[i], 0, 0)

    return pl.pallas_call(
        _expert_kernel,
        grid_spec=pltpu.PrefetchScalarGridSpec(
            num_scalar_prefetch=3,
            grid=(n_slots // EXPERT_TM,),
            in_specs=[
                pl.BlockSpec(memory_space=pl.ANY),
                pl.BlockSpec((1, 1, d, f), wsel),
                pl.BlockSpec((1, 1, d, f), wsel),
                pl.BlockSpec((1, 1, f, d), wsel),
            ],
            out_specs=pl.BlockSpec((EXPERT_TM, d), lambda i, src_r, te, nv: (i, 0)),
            scratch_shapes=[pltpu.VMEM((2, EXPERT_TM, d), F32),
                            pltpu.VMEM((d, f), BF16), pltpu.VMEM((d, f), BF16),
                            pltpu.VMEM((f, d), BF16), pltpu.SemaphoreType.DMA((2,))],
        ),
        out_shape=jax.ShapeDtypeStruct((n_slots, d), F32),
        compiler_params=_cparams(1, 52),
        name="moe_experts",
    )(src, tile_expert, n_live, hn, w_gate, w_up, w_down)


def _combine_kernel(pos_ref, h_ref, w1_ref, w2_ref, fw_ref, ys_hbm, o_ref, ybuf, sem,
                    *, n_tok, final_norm):
    tm = h_ref.shape[0]
    base = pl.program_id(0) * tm

    def row_dma(slot, i, src_row):
        return pltpu.make_async_copy(ys_hbm.at[pl.ds(src_row, 1)], ybuf.at[slot, pl.ds(i, 1)], sem)

    def issue(i, carry):
        row_dma(0, i, pos_ref[base + i]).start()
        row_dma(1, i, pos_ref[n_tok + base + i]).start()
        return carry

    def drain(i, carry):
        row_dma(0, 0, 0).wait()
        return carry

    lax.fori_loop(0, tm, issue, 0)
    lax.fori_loop(0, 2 * tm, drain, 0)
    out = h_ref[...] + w1_ref[...] * ybuf[0] + w2_ref[...] * ybuf[1]
    if final_norm:
        var = jnp.mean(out * out, axis=-1, keepdims=True)
        out = (out * lax.rsqrt(var + EPS)) * fw_ref[...]
    o_ref[...] = out


def _combine(h, ys, pos, w1, w2, final_w, final_norm):
    t, d = h.shape
    tm = min(COMBINE_TM, t)
    return pl.pallas_call(
        functools.partial(_combine_kernel, n_tok=t, final_norm=final_norm),
        grid_spec=pltpu.PrefetchScalarGridSpec(
            num_scalar_prefetch=1,
            grid=(t // tm,),
            in_specs=[
                pl.BlockSpec((tm, d), lambda i, p: (i, 0)),
                pl.BlockSpec((tm, 1), lambda i, p: (i, 0)),
                pl.BlockSpec((tm, 1), lambda i, p: (i, 0)),
                pl.BlockSpec((1, d), lambda i, p: (0, 0)),
                pl.BlockSpec(memory_space=pl.ANY),
            ],
            out_specs=pl.BlockSpec((tm, d), lambda i, p: (i, 0)),
            scratch_shapes=[pltpu.VMEM((2, tm, d), F32), pltpu.SemaphoreType.DMA(())],
        ),
        out_shape=jax.ShapeDtypeStruct((t, d), F32),
        compiler_params=_cparams(1, 32),
        name="moe_combine",
    )(pos, h, w1.reshape(t, 1), w2.reshape(t, 1), final_w.reshape(1, d), ys)


def _moe(h, nw, w_group, b_group, w_router, b_router, w_gate, w_up, w_down, layer, final_w,
         final_norm):
    t, d = h.shape
    hn, idx, gw, cnt = _route(h, nw, w_group, b_group, w_router, b_router)
    counts = cnt[:, 0]
    padded = ((counts + EXPERT_TM - 1) // EXPERT_TM) * EXPERT_TM
    seg_end = jnp.cumsum(padded)
    seg_start = seg_end - padded
    pos = jnp.concatenate([seg_start[idx[0]] + idx[2], seg_start[idx[1]] + idx[3]]).astype(I32)
    n_tiles = (2 * t) // EXPERT_TM + N_EXPERTS
    n_live = (seg_end[-1] // EXPERT_TM).astype(I32)
    tile_first = jnp.minimum(jnp.arange(n_tiles, dtype=I32), n_live - 1) * EXPERT_TM
    tile_expert = jnp.sum(tile_first[:, None] >= seg_end[None, :], axis=1).astype(I32)
    tok = jnp.arange(t, dtype=I32)
    src = jnp.zeros((n_tiles * EXPERT_TM,), I32).at[pos].set(jnp.concatenate([tok, tok]))
    ys = _experts(hn, src, tile_expert, n_live.reshape(1), w_gate, w_up, w_down, layer)
    return _combine(h, ys, pos, gw[0], gw[1], final_w, final_norm)


def kernel(x, norm_mix_w, norm_ffn_w, final_norm_w, mix_w_in, pool_w, pool_scale, attn_sinks,
           mix_w_out, ssm_w_in, ssm_conv_w, ssm_conv_b, ssm_dt_bias, ssm_a_log, ssm_d,
           ssm_norm_w, ssm_w_out, moe_w_group, moe_b_group, moe_w_router, moe_b_router,
           moe_w_gate, moe_w_up, moe_w_down):
    batch, seq_len, d = x.shape
    h = x.reshape(batch * seq_len, d)

    proj = _norm_matmul(h, norm_mix_w[0], mix_w_in[0], MIX_IN_WIDTH, PROJ_TN, BF16)
    mixed = _pool_attn(proj, pool_w[0], pool_scale[0], attn_sinks[0], seq_len)
    h = _matmul_residual(mixed, mix_w_out[0], h, PROJ_TN)
    h = _moe(h, norm_ffn_w[0], moe_w_group[0], moe_b_group[0], moe_w_router[0], moe_b_router[0],
             moe_w_gate, moe_w_up, moe_w_down, 0, final_norm_w, False)

    zx = _norm_matmul(h, norm_mix_w[1], ssm_w_in[0], ZX_WIDTH, PROJ_TN, BF16)
    dt_raw = _norm_matmul(h, norm_mix_w[1], ssm_w_in[0][:, ZX_WIDTH:], N_SSM_HEADS, N_SSM_HEADS, F32)
    y = _ssd(zx, dt_raw, ssm_conv_w[0], ssm_conv_b[0], ssm_dt_bias[0], ssm_a_log[0], ssm_d[0],
             ssm_norm_w[0], batch, seq_len)
    h = _matmul_residual(y, ssm_w_out[0], h, PROJ_TN // 2)
    h = _moe(h, norm_ffn_w[1], moe_w_group[1], moe_b_group[1], moe_w_router[1], moe_b_router[1],
             moe_w_gate, moe_w_up, moe_w_down, 1, final_norm_w, True)
    return h.reshape(batch, seq_len, d)
```

```python
import functools

import jax
import jax.numpy as jnp
from jax import lax
from jax.experimental import pallas as pl
from jax.experimental.pallas import tpu as pltpu

F32 = jnp.float32
BF16 = jnp.bfloat16
I32 = jnp.int32

D_MODEL = 2048
EPS = 1e-6
POOL_WINDOWS = (2, 4, 8, 16)
POOL_WIDTH = 1024
POOL_GROUP = 256
HEAD_DIM = 64
N_Q_HEADS = 16
N_KV_HEADS = 4
Q_PER_KV = 4
ATTN_WIDTH = 1024
KV_WIDTH = 256
WINDOW = 128
BLOCK = 128
MIX_IN_WIDTH = 2560
D_INNER = 4096
SSM_HEAD_DIM = 64
N_SSM_HEADS = 64
N_SSM_GROUPS = 8
HEADS_PER_GROUP = 8
D_STATE = 128
CONV_WIDTH = 4
CHUNK = 128
CONV_CH = D_INNER + 2 * N_SSM_GROUPS * D_STATE
ZX_WIDTH = D_INNER + CONV_CH
N_EXPERT_GROUPS = 4
EXPERTS_PER_GROUP = 4
N_EXPERTS = 16
D_FF_EXPERT = 512
SLOPES = tuple(float(2.0 ** (-8.0 * (i + 1) / N_Q_HEADS)) for i in range(N_Q_HEADS))

V7X_LANES = 128
V7X_SUBLANES = 8

PROJ_TM = 1024
NORM_ROWS = 128
ROUTE_TM = 512
EXPERT_TM = 256
COMBINE_TM = 256
ROUTE_ROWS = 32
EXPERT_ROW0 = 8
SLAB = D_MODEL // V7X_LANES


def _cparams(n_axes, vmem_mb):
    return pltpu.CompilerParams(
        dimension_semantics=("arbitrary",) * n_axes,
        vmem_limit_bytes=vmem_mb * 1024 * 1024,
    )


def _silu(x):
    half = 0.5 * x
    return half + half * jnp.tanh(half)


def _split_bf16(x):
    hi = x.astype(BF16)
    lo = (x - hi.astype(F32)).astype(BF16)
    return hi, lo


def _slab_rows(j, n):
    return pl.ds(j, n, stride=SLAB)


def _norm_matmul_kernel(x_ref, nw_ref, w_ref, o_ref, xn_ref, *, w_transposed):
    tm = x_ref.shape[0]

    @pl.when(pl.program_id(1) == 0)
    def _():
        def body(r, carry):
            rows = pl.ds(pl.multiple_of(r * NORM_ROWS, NORM_ROWS), NORM_ROWS)
            x = x_ref[rows, :]
            var = jnp.mean(x * x, axis=-1, keepdims=True)
            xn_ref[rows, :] = ((x * lax.rsqrt(var + EPS)) * nw_ref[...]).astype(BF16)
            return carry

        lax.fori_loop(0, tm // NORM_ROWS, body, 0)

    contract = ((1,), (1,)) if w_transposed else ((1,), (0,))
    o_ref[...] = lax.dot_general(xn_ref[...], w_ref[0].astype(BF16), (contract, ((), ())),
                                 preferred_element_type=F32).astype(o_ref.dtype)


def _norm_matmul(x, nw, w, n_out, tn, out_dtype, w_transposed=False):
    t, d = x.shape
    tm = min(PROJ_TM, t)
    if w_transposed:
        w_spec = pl.BlockSpec((1, tn, d), lambda i, j: (0, j, 0))
    else:
        w_spec = pl.BlockSpec((1, d, tn), lambda i, j: (0, 0, j))
    return pl.pallas_call(
        functools.partial(_norm_matmul_kernel, w_transposed=w_transposed),
        grid=(t // tm, n_out // tn),
        in_specs=[
            pl.BlockSpec((tm, d), lambda i, j: (i, 0)),
            pl.BlockSpec((1, d), lambda i, j: (0, 0)),
            w_spec,
        ],
        out_specs=pl.BlockSpec((tm, tn), lambda i, j: (i, j)),
        out_shape=jax.ShapeDtypeStruct((t, n_out), out_dtype),
        scratch_shapes=[pltpu.VMEM((tm, d), BF16)],
        compiler_params=_cparams(2, 56),
        name="norm_matmul",
    )(x, nw.reshape(1, d), w)


def _matmul_res_kernel(a_ref, w_ref, r_ref, o_ref):
    o_ref[...] = r_ref[...] + jnp.dot(a_ref[...], w_ref[0].astype(BF16),
                                      preferred_element_type=F32)


def _matmul_residual(a, w, res, tn):
    t, k = a.shape
    n = w.shape[2]
    tm = min(PROJ_TM, t)
    return pl.pallas_call(
        _matmul_res_kernel,
        grid=(t // tm, n // tn),
        in_specs=[
            pl.BlockSpec((tm, k), lambda i, j: (i, 0)),
            pl.BlockSpec((1, k, tn), lambda i, j: (0, 0, j)),
            pl.BlockSpec((tm, tn), lambda i, j: (i, j)),
        ],
        out_specs=pl.BlockSpec((tm, tn), lambda i, j: (i, j)),
        out_shape=jax.ShapeDtypeStruct((t, n), F32),
        compiler_params=_cparams(2, 56),
        name="matmul_residual",
    )(a, w, res)


def _pool_attn_kernel(sinks_ref, u_ref, up_ref, q_ref, k_ref, kp_ref, v_ref, vp_ref,
                      pw_ref, ps_ref, o_ref, *, blocks_per_seq):
    blk = pl.program_id(0) % blocks_per_seq
    first = blk == 0
    row = lax.broadcasted_iota(I32, (BLOCK, 2 * BLOCK), 0)
    col = lax.broadcasted_iota(I32, (BLOCK, 2 * BLOCK), 1)
    dist = row + BLOCK - col

    u_cur = u_ref[...]
    u_prev = jnp.where(first, jnp.zeros_like(u_cur), up_ref[...])
    u_ext = jnp.concatenate([u_prev, u_cur], axis=0)
    pos = blk * BLOCK + lax.broadcasted_iota(I32, (BLOCK, 1), 0)
    for g, w in enumerate(POOL_WINDOWS):
        cols = slice(g * POOL_GROUP, (g + 1) * POOL_GROUP)
        band = jnp.where((dist >= 0) & (dist < w), 1.0, 0.0).astype(BF16)
        wsum = jnp.dot(band, u_ext[:, cols], preferred_element_type=F32)
        count = jnp.minimum(pos + 1, w).astype(F32)
        pooled = (wsum / count - u_cur[:, cols].astype(F32)).astype(BF16)
        out_g = jnp.dot(pooled, pw_ref[0, g].astype(BF16), preferred_element_type=F32)
        o_ref[:, cols] = (out_g * ps_ref[:, cols]).astype(o_ref.dtype)

    kk = jnp.concatenate([kp_ref[...], k_ref[...]], axis=0)
    vv = jnp.concatenate([vp_ref[...], v_ref[...]], axis=0)
    valid = (dist >= 0) & (dist < WINDOW) & ((col >= BLOCK) | jnp.logical_not(first))
    dist_f = dist.astype(F32)
    scale = HEAD_DIM ** -0.5
    for pair in range(N_Q_HEADS // 2):
        q_pair = q_ref[:, pair * 2 * HEAD_DIM:(pair + 1) * 2 * HEAD_DIM]
        outs = []
        for sub in range(2):
            h = pair * 2 + sub
            g = h // Q_PER_KV
            qh = q_pair[:, sub * HEAD_DIM:(sub + 1) * HEAD_DIM]
            kg = kk[:, g * HEAD_DIM:(g + 1) * HEAD_DIM]
            vg = vv[:, g * HEAD_DIM:(g + 1) * HEAD_DIM]
            s = lax.dot_general(qh, kg, (((1,), (1,)), ((), ())),
                                preferred_element_type=F32) * scale
            s = jnp.where(valid, s - SLOPES[h] * dist_f, -jnp.inf)
            sink = sinks_ref[0, h]
            m = jnp.maximum(jnp.max(s, axis=-1, keepdims=True), sink)
            p = jnp.exp(s - m)
            denom = jnp.sum(p, axis=-1, keepdims=True) + jnp.exp(sink - m)
            probs = (p / denom).astype(BF16)
            outs.append(jnp.dot(probs, vg, preferred_element_type=F32))
        lo = ATTN_WIDTH + pair * 2 * HEAD_DIM
        o_ref[:, lo:lo + 2 * HEAD_DIM] = jnp.concatenate(outs, axis=1).astype(o_ref.dtype)


def _pool_attn(proj, pool_w, pool_scale, sinks, seq_len):
    t = proj.shape[0]
    nblk = t // BLOCK
    kcol = (POOL_WIDTH + ATTN_WIDTH) // KV_WIDTH

    def cur(cb):
        return lambda i: (i, cb)

    def prev(cb):
        return lambda i: (jnp.maximum(i - 1, 0), cb)

    return pl.pallas_call(
        functools.partial(_pool_attn_kernel, blocks_per_seq=seq_len // BLOCK),
        grid=(nblk,),
        in_specs=[
            pl.BlockSpec(memory_space=pltpu.SMEM),
            pl.BlockSpec((BLOCK, POOL_WIDTH), cur(0)),
            pl.BlockSpec((BLOCK, POOL_WIDTH), prev(0)),
            pl.BlockSpec((BLOCK, ATTN_WIDTH), cur(1)),
            pl.BlockSpec((BLOCK, KV_WIDTH), cur(kcol)),
            pl.BlockSpec((BLOCK, KV_WIDTH), prev(kcol)),
            pl.BlockSpec((BLOCK, KV_WIDTH), cur(kcol + 1)),
            pl.BlockSpec((BLOCK, KV_WIDTH), prev(kcol + 1)),
            pl.BlockSpec((1, len(POOL_WINDOWS), POOL_GROUP, POOL_GROUP), lambda i: (0, 0, 0, 0)),
            pl.BlockSpec((1, POOL_WIDTH), lambda i: (0, 0)),
        ],
        out_specs=pl.BlockSpec((BLOCK, POOL_WIDTH + ATTN_WIDTH), lambda i: (i, 0)),
        out_shape=jax.ShapeDtypeStruct((t, POOL_WIDTH + ATTN_WIDTH), BF16),
        compiler_params=_cparams(1, 32),
        name="pool_attn",
    )(sinks, proj, proj, proj, proj, proj, proj, proj, pool_w, pool_scale)


def _ssd_kernel(z_ref, x_ref, bc_ref, dt_ref, cw_ref, cb_ref, dtb_ref, alog_ref, dexp_ref,
                nw_ref, e2_ref, o_ref, prev_ref, state_ref, y_ref):
    gn = N_SSM_GROUPS * D_STATE
    taps = CONV_WIDTH - 1

    @pl.when(pl.program_id(1) == 0)
    def _():
        prev_ref[...] = jnp.zeros_like(prev_ref)
        state_ref[...] = jnp.zeros_like(state_ref)

    cur = jnp.concatenate([x_ref[...], bc_ref[...]], axis=1)
    cat = jnp.concatenate([prev_ref[...], cur], axis=0)
    prev_ref[...] = cur
    srow = lax.broadcasted_iota(I32, (taps * CHUNK, 2 * CHUNK), 0)
    scol = lax.broadcasted_iota(I32, (taps * CHUNK, 2 * CHUNK), 1)
    shift = jnp.where(scol + taps - srow // CHUNK == srow % CHUNK + CHUNK, 1.0, 0.0).astype(BF16)
    shifted = jnp.dot(shift, cat, preferred_element_type=F32)
    conv = cb_ref[...] + cw_ref[taps:taps + 1, :] * cur.astype(F32)
    for j in range(taps):
        conv = conv + cw_ref[j:j + 1, :] * shifted[j * CHUNK:(j + 1) * CHUNK, :]
    act = _silu(conv)
    xs = act[:, 0:D_INNER]
    xs_b = xs.astype(BF16)
    bm = act[:, D_INNER:D_INNER + gn]
    cm = act[:, D_INNER + gn:CONV_CH]

    dt_in = dt_ref[...] + dtb_ref[...]
    dt = jnp.maximum(dt_in, 0.0) + jnp.log(1.0 + jnp.exp(-jnp.abs(dt_in)))
    da = dt * (-jnp.exp(alog_ref[...]))
    row = lax.broadcasted_iota(I32, (CHUNK, CHUNK), 0)
    col = lax.broadcasted_iota(I32, (CHUNK, CHUNK), 1)
    causal = row >= col
    tri = jnp.where(causal, 1.0, 0.0).astype(BF16)
    da_hi, da_lo = _split_bf16(da)
    a_cum = (jnp.dot(tri, da_hi, preferred_element_type=F32)
             + jnp.dot(tri, da_lo, preferred_element_type=F32))
    e_cum = jnp.exp(a_cum)
    tr = jnp.concatenate([a_cum, dt], axis=1).T
    a_cum_t = tr[0:N_SSM_HEADS, :]
    dt_t = tr[N_SSM_HEADS:2 * N_SSM_HEADS, :]
    a_last_col = a_cum_t[:, CHUNK - 1:CHUNK]
    w_t = dt_t * jnp.exp(a_last_col - a_cum_t)
    dec8 = jnp.broadcast_to(e_cum[CHUNK - 1:CHUNK, :], (V7X_SUBLANES, N_SSM_HEADS))
    dec_hi, dec_lo = _split_bf16(dec8)
    dec_exp = (jnp.dot(dec_hi, e2_ref[...], preferred_element_type=F32)
               + jnp.dot(dec_lo, e2_ref[...], preferred_element_type=F32))[0:1, :]

    lane = lax.broadcasted_iota(I32, (CHUNK, 2 * SSM_HEAD_DIM), 1)
    left = lane < SSM_HEAD_DIM
    for g in range(N_SSM_GROUPS):
        b_g = bm[:, g * D_STATE:(g + 1) * D_STATE]
        c_g = cm[:, g * D_STATE:(g + 1) * D_STATE]
        bt_g = b_g.T
        cb_g = lax.dot_general(c_g.astype(BF16), b_g.astype(BF16),
                               (((1,), (1,)), ((), ())), preferred_element_type=F32)
        for k in range(HEADS_PER_GROUP // 2):
            slab = slice((g * 4 + k) * 2 * SSM_HEAD_DIM, (g * 4 + k + 1) * 2 * SSM_HEAD_DIM)
            lhs_y = []
            lhs_s = []
            for sub in range(2):
                h = g * HEADS_PER_GROUP + 2 * k + sub
                colb = jnp.broadcast_to(a_cum[:, h:h + 1], (CHUNK, CHUNK))
                rowb = jnp.broadcast_to(a_cum_t[h:h + 1, :], (CHUNK, CHUNK))
                decay = jnp.exp(jnp.where(causal, colb - rowb, -jnp.inf))
                m_h = cb_g * decay * jnp.broadcast_to(dt_t[h:h + 1, :], (CHUNK, CHUNK))
                lhs_y.append(m_h.astype(BF16))
                lhs_y.append((c_g * jnp.broadcast_to(e_cum[:, h:h + 1], (CHUNK, CHUNK))).astype(BF16))
                lhs_s.append((bt_g * jnp.broadcast_to(w_t[h:h + 1, :], (CHUNK, CHUNK))).astype(BF16))
            x_slab = xs_b[:, slab]
            s_slab = state_ref[:, slab]
            s_slab_b = s_slab.astype(BF16)
            zero = jnp.zeros_like(x_slab)
            x_l = jnp.where(left, x_slab, zero)
            x_r = jnp.where(left, zero, x_slab)
            s_l = jnp.where(left, s_slab_b, zero)
            s_r = jnp.where(left, zero, s_slab_b)
            y_pair = jnp.dot(jnp.concatenate(lhs_y, axis=1),
                             jnp.concatenate([x_l, s_l, x_r, s_r], axis=0),
                             preferred_element_type=F32)
            y_ref[:, slab] = y_pair
            d_state = jnp.dot(jnp.concatenate(lhs_s, axis=1),
                              jnp.concatenate([x_l, x_r], axis=0),
                              preferred_element_type=F32)
            state_ref[:, slab] = s_slab * dec_exp[:, slab] + d_state

    gsz = D_INNER // N_SSM_GROUPS
    for g in range(N_SSM_GROUPS):
        cols = slice(g * gsz, (g + 1) * gsz)
        y = y_ref[:, cols] + dexp_ref[:, cols] * xs[:, cols]
        gated = y * _silu(z_ref[:, cols].astype(F32))
        var = jnp.mean(gated * gated, axis=-1, keepdims=True)
        o_ref[:, cols] = ((gated * lax.rsqrt(var + EPS)) * nw_ref[:, cols]).astype(o_ref.dtype)


def _ssd(zx, dt_raw, conv_w, conv_b, dt_bias, a_log, d_skip, norm_w, batch, seq_len):
    t = zx.shape[0]
    nc = seq_len // CHUNK
    d_exp = jnp.repeat(d_skip, SSM_HEAD_DIM, axis=1)
    e2 = jnp.repeat(jnp.eye(N_SSM_HEADS, dtype=BF16), SSM_HEAD_DIM, axis=1)
    bc_col = (2 * D_INNER) // (2 * N_SSM_GROUPS * D_STATE)

    def rows(cb):
        return lambda b, c: (b * nc + c, cb)

    def whole(b, c):
        return (0, 0)

    return pl.pallas_call(
        _ssd_kernel,
        grid=(batch, nc),
        in_specs=[
            pl.BlockSpec((CHUNK, D_INNER), rows(0)),
            pl.BlockSpec((CHUNK, D_INNER), rows(1)),
            pl.BlockSpec((CHUNK, 2 * N_SSM_GROUPS * D_STATE), rows(bc_col)),
            pl.BlockSpec((CHUNK, N_SSM_HEADS), rows(0)),
            pl.BlockSpec((None, CONV_WIDTH, CONV_CH), lambda b, c: (0, 0, 0)),
            pl.BlockSpec((1, CONV_CH), whole),
            pl.BlockSpec((1, N_SSM_HEADS), whole),
            pl.BlockSpec((1, N_SSM_HEADS), whole),
            pl.BlockSpec((1, D_INNER), whole),
            pl.BlockSpec((1, D_INNER), whole),
            pl.BlockSpec((N_SSM_HEADS, D_INNER), whole),
        ],
        out_specs=pl.BlockSpec((CHUNK, D_INNER), rows(0)),
        out_shape=jax.ShapeDtypeStruct((t, D_INNER), BF16),
        scratch_shapes=[
            pltpu.VMEM((CHUNK, CONV_CH), BF16),
            pltpu.VMEM((D_STATE, D_INNER), F32),
            pltpu.VMEM((CHUNK, D_INNER), F32),
        ],
        compiler_params=_cparams(2, 48),
        name="ssd",
    )(zx, zx, zx, dt_raw, conv_w, conv_b, dt_bias, a_log, d_exp, norm_w, e2)


def _route_kernel(h_ref, nw_ref, wt_ref, bias_ref, hn_ref, idx_ref, gw_ref, cnt_ref, carry_ref):
    tm = h_ref.shape[0]

    @pl.when(pl.program_id(0) == 0)
    def _():
        carry_ref[...] = jnp.zeros_like(carry_ref)

    h = h_ref[...]
    var = jnp.mean(h * h, axis=-1, keepdims=True)
    hn = (h * lax.rsqrt(var + EPS)) * nw_ref[...]
    for j in range(SLAB):
        hn_ref[_slab_rows(j, tm), :] = hn[:, j * V7X_LANES:(j + 1) * V7X_LANES]

    nt = (((1,), (1,)), ((), ()))
    hn_hi, hn_lo = _split_bf16(hn)
    w_hi, w_lo = _split_bf16(wt_ref[...])
    logits = (lax.dot_general(w_hi, hn_hi, nt, preferred_element_type=F32)
              + lax.dot_general(w_hi, hn_lo, nt, preferred_element_type=F32)
              + lax.dot_general(w_lo, hn_hi, nt, preferred_element_type=F32)
              + bias_ref[...])

    def lrow(r):
        return logits[r:r + 1, :]

    gl = [lrow(g) for g in range(N_EXPERT_GROUPS)]
    gmax = functools.reduce(jnp.maximum, gl)
    gidx = jnp.full((1, tm), N_EXPERT_GROUPS - 1, I32)
    for g in range(N_EXPERT_GROUPS - 2, -1, -1):
        gidx = jnp.where(gl[g] == gmax, g, gidx)
    gsum = functools.reduce(jnp.add, [jnp.exp(x - gmax) for x in gl])
    g_weight = 1.0 / gsum

    el = []
    for e in range(EXPERTS_PER_GROUP):
        v = lrow(EXPERT_ROW0 + e)
        for g in range(1, N_EXPERT_GROUPS):
            v = jnp.where(gidx == g, lrow(EXPERT_ROW0 + g * EXPERTS_PER_GROUP + e), v)
        el.append(v)

    def first_argmax(vals):
        m = functools.reduce(jnp.maximum, vals)
        idx = jnp.full((1, tm), len(vals) - 1, I32)
        for e in range(len(vals) - 2, -1, -1):
            idx = jnp.where(vals[e] == m, e, idx)
        return m, idx

    m1, i1 = first_argmax(el)
    el2 = [jnp.where(i1 == e, -jnp.inf, el[e]) for e in range(EXPERTS_PER_GROUP)]
    m2, i2 = first_argmax(el2)
    e21 = jnp.exp(m2 - m1)
    w1 = g_weight / (1.0 + e21)
    w2 = g_weight * e21 / (1.0 + e21)
    eid1 = gidx * EXPERTS_PER_GROUP + i1
    eid2 = gidx * EXPERTS_PER_GROUP + i2

    erow = lax.broadcasted_iota(I32, (N_EXPERTS, tm), 0)
    oh1 = erow == eid1
    oh2 = erow == eid2
    member = jnp.where(oh1 | oh2, 1.0, 0.0)
    s_i = lax.broadcasted_iota(I32, (tm, tm), 0)
    t_i = lax.broadcasted_iota(I32, (tm, tm), 1)
    before = jnp.where(s_i < t_i, 1.0, 0.0).astype(BF16)
    excl = jnp.dot(member.astype(BF16), before, preferred_element_type=F32)
    carry = carry_ref[...]
    tot = excl + carry[:, 0:1]
    rank1 = jnp.sum(jnp.where(oh1, tot, 0.0), axis=0, keepdims=True)
    rank2 = jnp.sum(jnp.where(oh2, tot, 0.0), axis=0, keepdims=True)
    carry = carry + jnp.sum(member, axis=1, keepdims=True)
    carry_ref[...] = carry
    cnt_ref[...] = carry.astype(I32)

    idx_ref[...] = jnp.zeros_like(idx_ref)
    idx_ref[0:1, :] = eid1
    idx_ref[1:2, :] = eid2
    idx_ref[2:3, :] = rank1.astype(I32)
    idx_ref[3:4, :] = rank2.astype(I32)
    gw_ref[...] = jnp.zeros_like(gw_ref)
    gw_ref[0:1, :] = w1
    gw_ref[1:2, :] = w2


def _route(h, nw, w_group, b_group, w_router, b_router):
    t, d = h.shape
    tm = min(ROUTE_TM, t)
    wt = jnp.zeros((ROUTE_ROWS, d), F32)
    wt = wt.at[0:N_EXPERT_GROUPS].set(w_group.T)
    wt = wt.at[EXPERT_ROW0:EXPERT_ROW0 + N_EXPERTS].set(
        jnp.transpose(w_router, (0, 2, 1)).reshape(N_EXPERTS, d))
    bias = jnp.zeros((ROUTE_ROWS, 1), F32)
    bias = bias.at[0:N_EXPERT_GROUPS, 0].set(b_group)
    bias = bias.at[EXPERT_ROW0:EXPERT_ROW0 + N_EXPERTS, 0].set(b_router.reshape(N_EXPERTS))
    return pl.pallas_call(
        _route_kernel,
        grid=(t // tm,),
        in_specs=[
            pl.BlockSpec((tm, d), lambda i: (i, 0)),
            pl.BlockSpec((1, d), lambda i: (0, 0)),
            pl.BlockSpec((ROUTE_ROWS, d), lambda i: (0, 0)),
            pl.BlockSpec((ROUTE_ROWS, 1), lambda i: (0, 0)),
        ],
        out_specs=[
            pl.BlockSpec((tm * SLAB, V7X_LANES), lambda i: (i, 0)),
            pl.BlockSpec((V7X_SUBLANES, tm), lambda i: (0, i)),
            pl.BlockSpec((V7X_SUBLANES, tm), lambda i: (0, i)),
            pl.BlockSpec((N_EXPERTS, V7X_LANES), lambda i: (0, 0)),
        ],
        out_shape=[
            jax.ShapeDtypeStruct((t * SLAB, V7X_LANES), F32),
            jax.ShapeDtypeStruct((V7X_SUBLANES, t), I32),
            jax.ShapeDtypeStruct((V7X_SUBLANES, t), F32),
            jax.ShapeDtypeStruct((N_EXPERTS, V7X_LANES), I32),
        ],
        scratch_shapes=[pltpu.VMEM((N_EXPERTS, V7X_LANES), F32)],
        compiler_params=_cparams(1, 40),
        name="moe_route",
    )(h, nw.reshape(1, d), wt, bias)


def _expert_kernel(src_ref, dst_ref, te_ref, nv_ref, hn_hbm, wg_ref, wu_ref, wd_ref, out_hbm,
                   xbuf, ybuf, wg_b, wu_b, wd_b, gsems, ssem, *, n_dst):
    i = pl.program_id(0)
    n_tiles = pl.num_programs(0)
    n_live = nv_ref[0]
    live = i < n_live
    tm = EXPERT_TM

    def slab_at(r):
        return pl.ds(pl.multiple_of(r * SLAB, SLAB), SLAB)

    def gather_copy(tile_slot, r, tok):
        return pltpu.make_async_copy(hn_hbm.at[slab_at(tok)], xbuf.at[tile_slot, slab_at(r)],
                                     gsems.at[tile_slot])

    def scatter_copy(r, dst_row):
        return pltpu.make_async_copy(ybuf.at[slab_at(r)], out_hbm.at[slab_at(dst_row)], ssem)

    def gather_start(tile):
        base = tile * tm

        def body(r, carry):
            gather_copy(tile % 2, r, src_ref[base + r]).start()
            return carry

        lax.fori_loop(0, tm, body, 0, unroll=8)

    def gather_wait(tile_slot):
        def body(r, carry):
            gather_copy(tile_slot, 0, 0).wait()
            return carry

        lax.fori_loop(0, tm, body, 0, unroll=8)

    def scatter_start(tile):
        base = tile * tm

        def body(r, carry):
            scatter_copy(r, dst_ref[base + r]).start()
            return carry

        lax.fori_loop(0, tm, body, 0, unroll=8)

    def scatter_wait():
        def body(r, carry):
            scatter_copy(0, 0).wait()
            return carry

        lax.fori_loop(0, tm, body, 0, unroll=8)

    def dump_copy():
        return pltpu.make_async_copy(ybuf, out_hbm.at[pl.ds((n_dst - tm) * SLAB, tm * SLAB)], ssem)

    @pl.when(i == 0)
    def _():
        ybuf[...] = jnp.zeros_like(ybuf)
        dump_copy().start()
        gather_start(i)
        dump_copy().wait()

    @pl.when(i + 1 < n_live)
    def _():
        gather_start(i + 1)

    changed = jnp.logical_or(i == 0, te_ref[i] != te_ref[jnp.maximum(i - 1, 0)])

    @pl.when(jnp.logical_and(live, changed))
    def _():
        wg_b[...] = wg_ref[0, 0].astype(BF16)
        wu_b[...] = wu_ref[0, 0].astype(BF16)
        wd_b[...] = wd_ref[0, 0].astype(BF16)

    @pl.when(live)
    def _():
        gather_wait(i % 2)
        x = jnp.concatenate([xbuf[i % 2, _slab_rows(j, tm), :].astype(BF16) for j in range(SLAB)],
                            axis=1)
        gate = jnp.dot(x, wg_b[...], preferred_element_type=F32)
        up = jnp.dot(x, wu_b[...], preferred_element_type=F32)
        act = (_silu(gate) * up).astype(BF16)
        y = jnp.dot(act, wd_b[...], preferred_element_type=F32)

        @pl.when(i > 0)
        def _():
            scatter_wait()

        for j in range(SLAB):
            ybuf[_slab_rows(j, tm), :] = y[:, j * V7X_LANES:(j + 1) * V7X_LANES]
        scatter_start(i)

    @pl.when(i == n_tiles - 1)
    def _():
        scatter_wait()


def _experts(hn, src, dst, tile_expert, n_live, w_gate, w_up, w_down, layer, n_dst):
    d = D_MODEL
    n_slots = src.shape[0]
    f = w_gate.shape[3]

    def wsel(i, src_r, dst_r, te, nv):
        return (layer, te[i], 0, 0)

    return pl.pallas_call(
        functools.partial(_expert_kernel, n_dst=n_dst),
        grid_spec=pltpu.PrefetchScalarGridSpec(
            num_scalar_prefetch=4,
            grid=(n_slots // EXPERT_TM,),
            in_specs=[
                pl.BlockSpec(memory_space=pl.ANY),
                pl.BlockSpec((1, 1, d, f), wsel),
                pl.BlockSpec((1, 1, d, f), wsel),
                pl.BlockSpec((1, 1, f, d), wsel),
            ],
            out_specs=pl.BlockSpec(memory_space=pl.ANY),
            scratch_shapes=[pltpu.VMEM((2, EXPERT_TM * SLAB, V7X_LANES), F32),
                            pltpu.VMEM((EXPERT_TM * SLAB, V7X_LANES), F32),
                            pltpu.VMEM((d, f), BF16), pltpu.VMEM((d, f), BF16),
                            pltpu.VMEM((f, d), BF16),
                            pltpu.SemaphoreType.DMA((2,)), pltpu.SemaphoreType.DMA(())],
        ),
        out_shape=jax.ShapeDtypeStruct((n_dst * SLAB, V7X_LANES), F32),
        compiler_params=pltpu.CompilerParams(dimension_semantics=("arbitrary",),
                                             vmem_limit_bytes=52 * 1024 * 1024,
                                             has_side_effects=True),
        name="moe_experts",
    )(src, dst, tile_expert, n_live, hn, w_gate, w_up, w_down)


def _combine_kernel(h_ref, w1_ref, w2_ref, fw_ref, y1_ref, y2_ref, o_ref, *, final_norm):
    tm = h_ref.shape[0]
    w1 = w1_ref[...]
    w2 = w2_ref[...]
    ssq = jnp.zeros((tm, 1), F32)
    for j in range(SLAB):
        cols = slice(j * V7X_LANES, (j + 1) * V7X_LANES)
        out = h_ref[:, cols] + w1 * y1_ref[_slab_rows(j, tm), :] + w2 * y2_ref[_slab_rows(j, tm), :]
        o_ref[:, cols] = out
        ssq = ssq + jnp.sum(out * out, axis=-1, keepdims=True)
    if final_norm:
        inv = lax.rsqrt(ssq / D_MODEL + EPS)
        o_ref[...] = (o_ref[...] * inv) * fw_ref[...]


def _combine(h, ys, w1, w2, final_w, final_norm):
    t, d = h.shape
    tm = min(COMBINE_TM, t)
    nt = t // tm
    return pl.pallas_call(
        functools.partial(_combine_kernel, final_norm=final_norm),
        grid=(nt,),
        in_specs=[
            pl.BlockSpec((tm, d), lambda i: (i, 0)),
            pl.BlockSpec((tm, 1), lambda i: (i, 0)),
            pl.BlockSpec((tm, 1), lambda i: (i, 0)),
            pl.BlockSpec((1, d), lambda i: (0, 0)),
            pl.BlockSpec((tm * SLAB, V7X_LANES), lambda i: (i, 0)),
            pl.BlockSpec((tm * SLAB, V7X_LANES), lambda i: (i + nt, 0)),
        ],
        out_specs=pl.BlockSpec((tm, d), lambda i: (i, 0)),
        out_shape=jax.ShapeDtypeStruct((t, d), F32),
        compiler_params=_cparams(1, 32),
        name="moe_combine",
    )(h, w1.reshape(t, 1), w2.reshape(t, 1), final_w.reshape(1, d), ys, ys)


def _moe(h, nw, w_group, b_group, w_router, b_router, w_gate, w_up, w_down, layer, final_w,
         final_norm):
    t, d = h.shape
    hn, idx, gw, cnt = _route(h, nw, w_group, b_group, w_router, b_router)
    counts = cnt[:, 0]
    padded = ((counts + EXPERT_TM - 1) // EXPERT_TM) * EXPERT_TM
    seg_end = jnp.cumsum(padded)
    seg_start = seg_end - padded
    pos = jnp.concatenate([seg_start[idx[0]] + idx[2], seg_start[idx[1]] + idx[3]]).astype(I32)
    n_tiles = (2 * t) // EXPERT_TM + N_EXPERTS
    n_slots = n_tiles * EXPERT_TM
    n_live = (seg_end[-1] // EXPERT_TM).astype(I32)
    tile_first = jnp.minimum(jnp.arange(n_tiles, dtype=I32), n_live - 1) * EXPERT_TM
    tile_expert = jnp.sum(tile_first[:, None] >= seg_end[None, :], axis=1).astype(I32)
    n_dst = 2 * t + EXPERT_TM
    spare = 2 * t + jnp.arange(n_slots, dtype=I32) % EXPERT_TM
    dst = spare.at[pos].set(jnp.arange(2 * t, dtype=I32))
    src = jnp.where(dst < 2 * t, dst % t, 0)
    ys = _experts(hn, src, dst, tile_expert, n_live.reshape(1), w_gate, w_up, w_down, layer, n_dst)
    return _combine(h, ys, gw[0], gw[1], final_w, final_norm)


def kernel(x, norm_mix_w, norm_ffn_w, final_norm_w, mix_w_in, pool_w, pool_scale, attn_sinks,
           mix_w_out, ssm_w_in, ssm_conv_w, ssm_conv_b, ssm_dt_bias, ssm_a_log, ssm_d,
           ssm_norm_w, ssm_w_out, moe_w_group, moe_b_group, moe_w_router, moe_b_router,
           moe_w_gate, moe_w_up, moe_w_down):
    batch, seq_len, d = x.shape
    h = x.reshape(batch * seq_len, d)

    proj = _norm_matmul(h, norm_mix_w[0], mix_w_in, MIX_IN_WIDTH, MIX_IN_WIDTH // 4, BF16)
    mixed = _pool_attn(proj, pool_w, pool_scale, attn_sinks, seq_len)
    h = _matmul_residual(mixed, mix_w_out, h, D_MODEL // 2)
    h = _moe(h, norm_ffn_w[0], moe_w_group[0], moe_b_group[0], moe_w_router[0], moe_b_router[0],
             moe_w_gate, moe_w_up, moe_w_down, 0, final_norm_w, False)

    ssm_w_in_t = jnp.swapaxes(ssm_w_in, 1, 2)
    zx = _norm_matmul(h, norm_mix_w[1], ssm_w_in_t, ZX_WIDTH, ZX_WIDTH // 10, BF16, True)
    dt_raw = _norm_matmul(h, norm_mix_w[1], ssm_w_in_t[:, ZX_WIDTH:, :], N_SSM_HEADS, N_SSM_HEADS,
                          F32, True)
    y = _ssd(zx, dt_raw, ssm_conv_w, ssm_conv_b, ssm_dt_bias, ssm_a_log, ssm_d, ssm_norm_w,
             batch, seq_len)
    h = _matmul_residual(y, ssm_w_out, h, D_MODEL // 4)
    h = _moe(h, norm_ffn_w[1], moe_w_group[1], moe_b_group[1], moe_w_router[1], moe_b_router[1],
             moe_w_gate, moe_w_up, moe_w_down, 1, final_norm_w, True)
    return h.reshape(batch, seq_len, d)
```

```python
import functools

import jax
import jax.numpy as jnp
from jax import lax
from jax.experimental import pallas as pl
from jax.experimental.pallas import tpu as pltpu

F32 = jnp.float32
BF16 = jnp.bfloat16
I32 = jnp.int32

D_MODEL = 2048
EPS = 1e-6
POOL_WINDOWS = (2, 4, 8, 16)
POOL_WIDTH = 1024
POOL_GROUP = 256
HEAD_DIM = 64
N_Q_HEADS = 16
N_KV_HEADS = 4
Q_PER_KV = 4
ATTN_WIDTH = 1024
KV_WIDTH = 256
WINDOW = 128
BLOCK = 128
MIX_IN_WIDTH = 2560
D_INNER = 4096
SSM_HEAD_DIM = 64
N_SSM_HEADS = 64
N_SSM_GROUPS = 8
HEADS_PER_GROUP = 8
D_STATE = 128
CONV_WIDTH = 4
CHUNK = 128
CONV_CH = D_INNER + 2 * N_SSM_GROUPS * D_STATE
ZX_WIDTH = D_INNER + CONV_CH
N_EXPERT_GROUPS = 4
EXPERTS_PER_GROUP = 4
N_EXPERTS = 16
D_FF_EXPERT = 512
SLOPES = tuple(float(2.0 ** (-8.0 * (i + 1) / N_Q_HEADS)) for i in range(N_Q_HEADS))

V7X_LANES = 128
V7X_SUBLANES = 8

PROJ_TM = 1024
NORM_ROWS = 128
ROUTE_TM = 512
EXPERT_TM = 256
COMBINE_TM = 256
ROUTE_ROWS = 32
EXPERT_ROW0 = 8
DMA_UNROLL = 32
HALF = D_MODEL // 2
SLAB = HALF // V7X_LANES


def _cparams(n_axes, vmem_mb):
    return pltpu.CompilerParams(
        dimension_semantics=("arbitrary",) * n_axes,
        vmem_limit_bytes=vmem_mb * 1024 * 1024,
    )


def _silu(x):
    half = 0.5 * x
    return half + half * jnp.tanh(half)


def _split_bf16(x):
    hi = x.astype(BF16)
    lo = (x - hi.astype(F32)).astype(BF16)
    return hi, lo


def _slab_rows(j, n):
    return pl.ds(j, n, stride=SLAB)


U32 = jnp.uint32


def _pack_bf16_pair(lo, hi):
    lo_bits = lax.bitcast_convert_type(lo.astype(BF16).astype(F32), U32)
    hi_bits = lax.bitcast_convert_type(hi.astype(BF16).astype(F32), U32)
    return (lo_bits >> 16) | hi_bits


def _unpack_bf16_pair(word):
    lo = lax.bitcast_convert_type(word << 16, F32)
    hi = lax.bitcast_convert_type(word & jnp.uint32(0xFFFF0000), F32)
    return lo, hi


def _slab_store(ref, x):
    n = x.shape[0]
    for j in range(SLAB):
        lo = x[:, j * V7X_LANES:(j + 1) * V7X_LANES]
        hi = x[:, HALF + j * V7X_LANES:HALF + (j + 1) * V7X_LANES]
        ref[_slab_rows(j, n), :] = _pack_bf16_pair(lo, hi)


def _slab_load(ref, n):
    pairs = [_unpack_bf16_pair(ref[_slab_rows(j, n), :]) for j in range(SLAB)]
    return [p[0] for p in pairs] + [p[1] for p in pairs]


def _norm_matmul_kernel(x_ref, nw_ref, w_ref, o_ref, xn_ref, *, w_transposed):
    tm = x_ref.shape[0]

    @pl.when(pl.program_id(1) == 0)
    def _():
        def body(r, carry):
            rows = pl.ds(pl.multiple_of(r * NORM_ROWS, NORM_ROWS), NORM_ROWS)
            x = x_ref[rows, :]
            var = jnp.mean(x * x, axis=-1, keepdims=True)
            xn_ref[rows, :] = ((x * lax.rsqrt(var + EPS)) * nw_ref[...]).astype(BF16)
            return carry

        lax.fori_loop(0, tm // NORM_ROWS, body, 0)

    contract = ((1,), (1,)) if w_transposed else ((1,), (0,))
    o_ref[...] = lax.dot_general(xn_ref[...], w_ref[0].astype(BF16), (contract, ((), ())),
                                 preferred_element_type=F32).astype(o_ref.dtype)


def _norm_matmul(x, nw, w, n_out, tn, out_dtype, w_transposed=False):
    t, d = x.shape
    tm = min(PROJ_TM, t)
    if w_transposed:
        w_spec = pl.BlockSpec((1, tn, d), lambda i, j: (0, j, 0))
    else:
        w_spec = pl.BlockSpec((1, d, tn), lambda i, j: (0, 0, j))
    return pl.pallas_call(
        functools.partial(_norm_matmul_kernel, w_transposed=w_transposed),
        grid=(t // tm, n_out // tn),
        in_specs=[
            pl.BlockSpec((tm, d), lambda i, j: (i, 0)),
            pl.BlockSpec((1, d), lambda i, j: (0, 0)),
            w_spec,
        ],
        out_specs=pl.BlockSpec((tm, tn), lambda i, j: (i, j)),
        out_shape=jax.ShapeDtypeStruct((t, n_out), out_dtype),
        scratch_shapes=[pltpu.VMEM((tm, d), BF16)],
        compiler_params=_cparams(2, 56),
        name="norm_matmul",
    )(x, nw.reshape(1, d), w)


def _matmul_res_kernel(a_ref, w_ref, r_ref, o_ref):
    o_ref[...] = r_ref[...] + jnp.dot(a_ref[...], w_ref[0].astype(BF16),
                                      preferred_element_type=F32)


def _matmul_residual(a, w, res, tn):
    t, k = a.shape
    n = w.shape[2]
    tm = min(PROJ_TM, t)
    return pl.pallas_call(
        _matmul_res_kernel,
        grid=(t // tm, n // tn),
        in_specs=[
            pl.BlockSpec((tm, k), lambda i, j: (i, 0)),
            pl.BlockSpec((1, k, tn), lambda i, j: (0, 0, j)),
            pl.BlockSpec((tm, tn), lambda i, j: (i, j)),
        ],
        out_specs=pl.BlockSpec((tm, tn), lambda i, j: (i, j)),
        out_shape=jax.ShapeDtypeStruct((t, n), F32),
        compiler_params=_cparams(2, 56),
        name="matmul_residual",
    )(a, w, res)


def _pool_attn_kernel(sinks_ref, u_ref, up_ref, q_ref, k_ref, kp_ref, v_ref, vp_ref,
                      pw_ref, ps_ref, o_ref, *, blocks_per_seq):
    blk = pl.program_id(0) % blocks_per_seq
    first = blk == 0
    row = lax.broadcasted_iota(I32, (BLOCK, 2 * BLOCK), 0)
    col = lax.broadcasted_iota(I32, (BLOCK, 2 * BLOCK), 1)
    dist = row + BLOCK - col

    u_cur = u_ref[...]
    u_prev = jnp.where(first, jnp.zeros_like(u_cur), up_ref[...])
    u_ext = jnp.concatenate([u_prev, u_cur], axis=0)
    pos = blk * BLOCK + lax.broadcasted_iota(I32, (BLOCK, 1), 0)
    for g, w in enumerate(POOL_WINDOWS):
        cols = slice(g * POOL_GROUP, (g + 1) * POOL_GROUP)
        band = jnp.where((dist >= 0) & (dist < w), 1.0, 0.0).astype(BF16)
        wsum = jnp.dot(band, u_ext[:, cols], preferred_element_type=F32)
        count = jnp.minimum(pos + 1, w).astype(F32)
        pooled = (wsum / count - u_cur[:, cols].astype(F32)).astype(BF16)
        out_g = jnp.dot(pooled, pw_ref[0, g].astype(BF16), preferred_element_type=F32)
        o_ref[:, cols] = (out_g * ps_ref[:, cols]).astype(o_ref.dtype)

    kk = jnp.concatenate([kp_ref[...], k_ref[...]], axis=0)
    vv = jnp.concatenate([vp_ref[...], v_ref[...]], axis=0)
    valid = (dist >= 0) & (dist < WINDOW) & ((col >= BLOCK) | jnp.logical_not(first))
    dist_f = dist.astype(F32)
    scale = HEAD_DIM ** -0.5
    for pair in range(N_Q_HEADS // 2):
        q_pair = q_ref[:, pair * 2 * HEAD_DIM:(pair + 1) * 2 * HEAD_DIM]
        outs = []
        for sub in range(2):
            h = pair * 2 + sub
            g = h // Q_PER_KV
            qh = q_pair[:, sub * HEAD_DIM:(sub + 1) * HEAD_DIM]
            kg = kk[:, g * HEAD_DIM:(g + 1) * HEAD_DIM]
            vg = vv[:, g * HEAD_DIM:(g + 1) * HEAD_DIM]
            s = lax.dot_general(qh, kg, (((1,), (1,)), ((), ())),
                                preferred_element_type=F32) * scale
            s = jnp.where(valid, s - SLOPES[h] * dist_f, -jnp.inf)
            sink = sinks_ref[0, h]
            m = jnp.maximum(jnp.max(s, axis=-1, keepdims=True), sink)
            p = jnp.exp(s - m)
            denom = jnp.sum(p, axis=-1, keepdims=True) + jnp.exp(sink - m)
            probs = (p / denom).astype(BF16)
            outs.append(jnp.dot(probs, vg, preferred_element_type=F32))
        lo = ATTN_WIDTH + pair * 2 * HEAD_DIM
        o_ref[:, lo:lo + 2 * HEAD_DIM] = jnp.concatenate(outs, axis=1).astype(o_ref.dtype)


def _pool_attn(proj, pool_w, pool_scale, sinks, seq_len):
    t = proj.shape[0]
    nblk = t // BLOCK
    kcol = (POOL_WIDTH + ATTN_WIDTH) // KV_WIDTH

    def cur(cb):
        return lambda i: (i, cb)

    def prev(cb):
        return lambda i: (jnp.maximum(i - 1, 0), cb)

    return pl.pallas_call(
        functools.partial(_pool_attn_kernel, blocks_per_seq=seq_len // BLOCK),
        grid=(nblk,),
        in_specs=[
            pl.BlockSpec(memory_space=pltpu.SMEM),
            pl.BlockSpec((BLOCK, POOL_WIDTH), cur(0)),
            pl.BlockSpec((BLOCK, POOL_WIDTH), prev(0)),
            pl.BlockSpec((BLOCK, ATTN_WIDTH), cur(1)),
            pl.BlockSpec((BLOCK, KV_WIDTH), cur(kcol)),
            pl.BlockSpec((BLOCK, KV_WIDTH), prev(kcol)),
            pl.BlockSpec((BLOCK, KV_WIDTH), cur(kcol + 1)),
            pl.BlockSpec((BLOCK, KV_WIDTH), prev(kcol + 1)),
            pl.BlockSpec((1, len(POOL_WINDOWS), POOL_GROUP, POOL_GROUP), lambda i: (0, 0, 0, 0)),
            pl.BlockSpec((1, POOL_WIDTH), lambda i: (0, 0)),
        ],
        out_specs=pl.BlockSpec((BLOCK, POOL_WIDTH + ATTN_WIDTH), lambda i: (i, 0)),
        out_shape=jax.ShapeDtypeStruct((t, POOL_WIDTH + ATTN_WIDTH), BF16),
        compiler_params=_cparams(1, 32),
        name="pool_attn",
    )(sinks, proj, proj, proj, proj, proj, proj, proj, pool_w, pool_scale)


def _ssd_kernel(z_ref, x_ref, bc_ref, dt_ref, cw_ref, cb_ref, dtb_ref, alog_ref, dexp_ref,
                nw_ref, e2_ref, o_ref, prev_ref, state_ref, y_ref):
    gn = N_SSM_GROUPS * D_STATE
    taps = CONV_WIDTH - 1

    @pl.when(pl.program_id(1) == 0)
    def _():
        prev_ref[...] = jnp.zeros_like(prev_ref)
        state_ref[...] = jnp.zeros_like(state_ref)

    cur = jnp.concatenate([x_ref[...], bc_ref[...]], axis=1)
    cat = jnp.concatenate([prev_ref[...], cur], axis=0)
    prev_ref[...] = cur
    srow = lax.broadcasted_iota(I32, (taps * CHUNK, 2 * CHUNK), 0)
    scol = lax.broadcasted_iota(I32, (taps * CHUNK, 2 * CHUNK), 1)
    shift = jnp.where(scol + taps - srow // CHUNK == srow % CHUNK + CHUNK, 1.0, 0.0).astype(BF16)
    shifted = jnp.dot(shift, cat, preferred_element_type=F32)
    conv = cb_ref[...] + cw_ref[taps:taps + 1, :] * cur.astype(F32)
    for j in range(taps):
        conv = conv + cw_ref[j:j + 1, :] * shifted[j * CHUNK:(j + 1) * CHUNK, :]
    act = _silu(conv)
    xs = act[:, 0:D_INNER]
    xs_b = xs.astype(BF16)
    bm = act[:, D_INNER:D_INNER + gn]
    cm = act[:, D_INNER + gn:CONV_CH]

    dt_in = dt_ref[...] + dtb_ref[...]
    dt = jnp.maximum(dt_in, 0.0) + jnp.log(1.0 + jnp.exp(-jnp.abs(dt_in)))
    da = dt * (-jnp.exp(alog_ref[...]))
    row = lax.broadcasted_iota(I32, (CHUNK, CHUNK), 0)
    col = lax.broadcasted_iota(I32, (CHUNK, CHUNK), 1)
    causal = row >= col
    tri = jnp.where(causal, 1.0, 0.0).astype(BF16)
    da_hi, da_lo = _split_bf16(da)
    a_cum = (jnp.dot(tri, da_hi, preferred_element_type=F32)
             + jnp.dot(tri, da_lo, preferred_element_type=F32))
    e_cum = jnp.exp(a_cum)
    tr = jnp.concatenate([a_cum, dt], axis=1).T
    a_cum_t = tr[0:N_SSM_HEADS, :]
    dt_t = tr[N_SSM_HEADS:2 * N_SSM_HEADS, :]
    a_last_col = a_cum_t[:, CHUNK - 1:CHUNK]
    w_t = dt_t * jnp.exp(a_last_col - a_cum_t)
    dec8 = jnp.broadcast_to(e_cum[CHUNK - 1:CHUNK, :], (V7X_SUBLANES, N_SSM_HEADS))
    dec_hi, dec_lo = _split_bf16(dec8)
    dec_exp = (jnp.dot(dec_hi, e2_ref[...], preferred_element_type=F32)
               + jnp.dot(dec_lo, e2_ref[...], preferred_element_type=F32))[0:1, :]

    lane = lax.broadcasted_iota(I32, (CHUNK, 2 * SSM_HEAD_DIM), 1)
    left = lane < SSM_HEAD_DIM
    for g in range(N_SSM_GROUPS):
        b_g = bm[:, g * D_STATE:(g + 1) * D_STATE]
        c_g = cm[:, g * D_STATE:(g + 1) * D_STATE]
        bt_g = b_g.T
        cb_g = lax.dot_general(c_g.astype(BF16), b_g.astype(BF16),
                               (((1,), (1,)), ((), ())), preferred_element_type=F32)
        for k in range(HEADS_PER_GROUP // 2):
            slab = slice((g * 4 + k) * 2 * SSM_HEAD_DIM, (g * 4 + k + 1) * 2 * SSM_HEAD_DIM)
            lhs_y = []
            lhs_s = []
            for sub in range(2):
                h = g * HEADS_PER_GROUP + 2 * k + sub
                colb = jnp.broadcast_to(a_cum[:, h:h + 1], (CHUNK, CHUNK))
                rowb = jnp.broadcast_to(a_cum_t[h:h + 1, :], (CHUNK, CHUNK))
                decay = jnp.exp(jnp.where(causal, colb - rowb, -jnp.inf))
                m_h = cb_g * decay * jnp.broadcast_to(dt_t[h:h + 1, :], (CHUNK, CHUNK))
                lhs_y.append(m_h.astype(BF16))
                lhs_y.append((c_g * jnp.broadcast_to(e_cum[:, h:h + 1], (CHUNK, CHUNK))).astype(BF16))
                lhs_s.append((bt_g * jnp.broadcast_to(w_t[h:h + 1, :], (CHUNK, CHUNK))).astype(BF16))
            x_slab = xs_b[:, slab]
            s_slab = state_ref[:, slab]
            s_slab_b = s_slab.astype(BF16)
            zero = jnp.zeros_like(x_slab)
            x_l = jnp.where(left, x_slab, zero)
            x_r = jnp.where(left, zero, x_slab)
            s_l = jnp.where(left, s_slab_b, zero)
            s_r = jnp.where(left, zero, s_slab_b)
            y_pair = jnp.dot(jnp.concatenate(lhs_y, axis=1),
                             jnp.concatenate([x_l, s_l, x_r, s_r], axis=0),
                             preferred_element_type=F32)
            y_ref[:, slab] = y_pair
            d_state = jnp.dot(jnp.concatenate(lhs_s, axis=1),
                              jnp.concatenate([x_l, x_r], axis=0),
                              preferred_element_type=F32)
            state_ref[:, slab] = s_slab * dec_exp[:, slab] + d_state

    gsz = D_INNER // N_SSM_GROUPS
    for g in range(N_SSM_GROUPS):
        cols = slice(g * gsz, (g + 1) * gsz)
        y = y_ref[:, cols] + dexp_ref[:, cols] * xs[:, cols]
        gated = y * _silu(z_ref[:, cols].astype(F32))
        var = jnp.mean(gated * gated, axis=-1, keepdims=True)
        o_ref[:, cols] = ((gated * lax.rsqrt(var + EPS)) * nw_ref[:, cols]).astype(o_ref.dtype)


def _ssd(zx, dt_raw, conv_w, conv_b, dt_bias, a_log, d_skip, norm_w, batch, seq_len):
    t = zx.shape[0]
    nc = seq_len // CHUNK
    d_exp = jnp.repeat(d_skip, SSM_HEAD_DIM, axis=1)
    e2 = jnp.repeat(jnp.eye(N_SSM_HEADS, dtype=BF16), SSM_HEAD_DIM, axis=1)
    bc_col = (2 * D_INNER) // (2 * N_SSM_GROUPS * D_STATE)

    def rows(cb):
        return lambda b, c: (b * nc + c, cb)

    def whole(b, c):
        return (0, 0)

    return pl.pallas_call(
        _ssd_kernel,
        grid=(batch, nc),
        in_specs=[
            pl.BlockSpec((CHUNK, D_INNER), rows(0)),
            pl.BlockSpec((CHUNK, D_INNER), rows(1)),
            pl.BlockSpec((CHUNK, 2 * N_SSM_GROUPS * D_STATE), rows(bc_col)),
            pl.BlockSpec((CHUNK, N_SSM_HEADS), rows(0)),
            pl.BlockSpec((None, CONV_WIDTH, CONV_CH), lambda b, c: (0, 0, 0)),
            pl.BlockSpec((1, CONV_CH), whole),
            pl.BlockSpec((1, N_SSM_HEADS), whole),
            pl.BlockSpec((1, N_SSM_HEADS), whole),
            pl.BlockSpec((1, D_INNER), whole),
            pl.BlockSpec((1, D_INNER), whole),
            pl.BlockSpec((N_SSM_HEADS, D_INNER), whole),
        ],
        out_specs=pl.BlockSpec((CHUNK, D_INNER), rows(0)),
        out_shape=jax.ShapeDtypeStruct((t, D_INNER), BF16),
        scratch_shapes=[
            pltpu.VMEM((CHUNK, CONV_CH), BF16),
            pltpu.VMEM((D_STATE, D_INNER), F32),
            pltpu.VMEM((CHUNK, D_INNER), F32),
        ],
        compiler_params=_cparams(2, 48),
        name="ssd",
    )(zx, zx, zx, dt_raw, conv_w, conv_b, dt_bias, a_log, d_exp, norm_w, e2)


def _route_kernel(h_ref, nw_ref, wt_ref, bias_ref, hn_ref, idx_ref, gw_ref, cnt_ref, carry_ref):
    tm = h_ref.shape[0]

    @pl.when(pl.program_id(0) == 0)
    def _():
        carry_ref[...] = jnp.zeros_like(carry_ref)

    h = h_ref[...]
    var = jnp.mean(h * h, axis=-1, keepdims=True)
    hn = (h * lax.rsqrt(var + EPS)) * nw_ref[...]
    _slab_store(hn_ref, hn)

    nt = (((1,), (1,)), ((), ()))
    hn_hi, hn_lo = _split_bf16(hn)
    w_hi, w_lo = _split_bf16(wt_ref[...])
    logits = (lax.dot_general(w_hi, hn_hi, nt, preferred_element_type=F32)
              + lax.dot_general(w_hi, hn_lo, nt, preferred_element_type=F32)
              + lax.dot_general(w_lo, hn_hi, nt, preferred_element_type=F32)
              + bias_ref[...])

    def lrow(r):
        return logits[r:r + 1, :]

    gl = [lrow(g) for g in range(N_EXPERT_GROUPS)]
    gmax = functools.reduce(jnp.maximum, gl)
    gidx = jnp.full((1, tm), N_EXPERT_GROUPS - 1, I32)
    for g in range(N_EXPERT_GROUPS - 2, -1, -1):
        gidx = jnp.where(gl[g] == gmax, g, gidx)
    gsum = functools.reduce(jnp.add, [jnp.exp(x - gmax) for x in gl])
    g_weight = 1.0 / gsum

    el = []
    for e in range(EXPERTS_PER_GROUP):
        v = lrow(EXPERT_ROW0 + e)
        for g in range(1, N_EXPERT_GROUPS):
            v = jnp.where(gidx == g, lrow(EXPERT_ROW0 + g * EXPERTS_PER_GROUP + e), v)
        el.append(v)

    def first_argmax(vals):
        m = functools.reduce(jnp.maximum, vals)
        idx = jnp.full((1, tm), len(vals) - 1, I32)
        for e in range(len(vals) - 2, -1, -1):
            idx = jnp.where(vals[e] == m, e, idx)
        return m, idx

    m1, i1 = first_argmax(el)
    el2 = [jnp.where(i1 == e, -jnp.inf, el[e]) for e in range(EXPERTS_PER_GROUP)]
    m2, i2 = first_argmax(el2)
    e21 = jnp.exp(m2 - m1)
    w1 = g_weight / (1.0 + e21)
    w2 = g_weight * e21 / (1.0 + e21)
    eid1 = gidx * EXPERTS_PER_GROUP + i1
    eid2 = gidx * EXPERTS_PER_GROUP + i2

    erow = lax.broadcasted_iota(I32, (N_EXPERTS, tm), 0)
    oh1 = erow == eid1
    oh2 = erow == eid2
    member = jnp.where(oh1 | oh2, 1.0, 0.0)
    s_i = lax.broadcasted_iota(I32, (tm, tm), 0)
    t_i = lax.broadcasted_iota(I32, (tm, tm), 1)
    before = jnp.where(s_i < t_i, 1.0, 0.0).astype(BF16)
    excl = jnp.dot(member.astype(BF16), before, preferred_element_type=F32)
    carry = carry_ref[...]
    tot = excl + carry[:, 0:1]
    rank1 = jnp.sum(jnp.where(oh1, tot, 0.0), axis=0, keepdims=True)
    rank2 = jnp.sum(jnp.where(oh2, tot, 0.0), axis=0, keepdims=True)
    carry = carry + jnp.sum(member, axis=1, keepdims=True)
    carry_ref[...] = carry
    cnt_ref[...] = carry.astype(I32)

    idx_ref[...] = jnp.zeros_like(idx_ref)
    idx_ref[0:1, :] = eid1
    idx_ref[1:2, :] = eid2
    idx_ref[2:3, :] = rank1.astype(I32)
    idx_ref[3:4, :] = rank2.astype(I32)
    gw_ref[...] = jnp.zeros_like(gw_ref)
    gw_ref[0:1, :] = w1
    gw_ref[1:2, :] = w2


def _route(h, nw, w_group, b_group, w_router, b_router):
    t, d = h.shape
    tm = min(ROUTE_TM, t)
    wt = jnp.zeros((ROUTE_ROWS, d), F32)
    wt = wt.at[0:N_EXPERT_GROUPS].set(w_group.T)
    wt = wt.at[EXPERT_ROW0:EXPERT_ROW0 + N_EXPERTS].set(
        jnp.transpose(w_router, (0, 2, 1)).reshape(N_EXPERTS, d))
    bias = jnp.zeros((ROUTE_ROWS, 1), F32)
    bias = bias.at[0:N_EXPERT_GROUPS, 0].set(b_group)
    bias = bias.at[EXPERT_ROW0:EXPERT_ROW0 + N_EXPERTS, 0].set(b_router.reshape(N_EXPERTS))
    return pl.pallas_call(
        _route_kernel,
        grid=(t // tm,),
        in_specs=[
            pl.BlockSpec((tm, d), lambda i: (i, 0)),
            pl.BlockSpec((1, d), lambda i: (0, 0)),
            pl.BlockSpec((ROUTE_ROWS, d), lambda i: (0, 0)),
            pl.BlockSpec((ROUTE_ROWS, 1), lambda i: (0, 0)),
        ],
        out_specs=[
            pl.BlockSpec((tm * SLAB, V7X_LANES), lambda i: (i, 0)),
            pl.BlockSpec((V7X_SUBLANES, tm), lambda i: (0, i)),
            pl.BlockSpec((V7X_SUBLANES, tm), lambda i: (0, i)),
            pl.BlockSpec((N_EXPERTS, V7X_LANES), lambda i: (0, 0)),
        ],
        out_shape=[
            jax.ShapeDtypeStruct((t * SLAB, V7X_LANES), U32),
            jax.ShapeDtypeStruct((V7X_SUBLANES, t), I32),
            jax.ShapeDtypeStruct((V7X_SUBLANES, t), F32),
            jax.ShapeDtypeStruct((N_EXPERTS, V7X_LANES), I32),
        ],
        scratch_shapes=[pltpu.VMEM((N_EXPERTS, V7X_LANES), F32)],
        compiler_params=_cparams(1, 40),
        name="moe_route",
    )(h, nw.reshape(1, d), wt, bias)


def _expert_kernel(src_ref, dst_ref, te_ref, nv_ref, hn_hbm, wg_ref, wu_ref, wd_ref, out_hbm,
                   xbuf, ybuf, wg_b, wu_b, wd_b, gsems, ssem, *, n_dst):
    i = pl.program_id(0)
    n_tiles = pl.num_programs(0)
    n_live = nv_ref[0]
    live = i < n_live
    tm = EXPERT_TM

    def slab_at(r):
        return pl.ds(pl.multiple_of(r * SLAB, SLAB), SLAB)

    def gather_copy(tile_slot, r, tok):
        return pltpu.make_async_copy(hn_hbm.at[slab_at(tok)], xbuf.at[tile_slot, slab_at(r)],
                                     gsems.at[tile_slot])

    def scatter_copy(r, dst_row):
        return pltpu.make_async_copy(ybuf.at[slab_at(r)], out_hbm.at[slab_at(dst_row)], ssem)

    def gather_start(tile):
        base = tile * tm

        def body(r, carry):
            gather_copy(tile % 2, r, src_ref[base + r]).start()
            return carry

        lax.fori_loop(0, tm, body, 0, unroll=DMA_UNROLL)

    def gather_wait(tile_slot):
        def body(r, carry):
            gather_copy(tile_slot, 0, 0).wait()
            return carry

        lax.fori_loop(0, tm, body, 0, unroll=DMA_UNROLL)

    def scatter_start(tile):
        base = tile * tm

        def body(r, carry):
            scatter_copy(r, dst_ref[base + r]).start()
            return carry

        lax.fori_loop(0, tm, body, 0, unroll=DMA_UNROLL)

    def scatter_wait():
        def body(r, carry):
            scatter_copy(0, 0).wait()
            return carry

        lax.fori_loop(0, tm, body, 0, unroll=DMA_UNROLL)

    def dump_copy():
        return pltpu.make_async_copy(ybuf, out_hbm.at[pl.ds((n_dst - tm) * SLAB, tm * SLAB)], ssem)

    @pl.when(i == 0)
    def _():
        ybuf[...] = jnp.zeros_like(ybuf)
        dump_copy().start()
        gather_start(i)
        dump_copy().wait()

    @pl.when(i + 1 < n_live)
    def _():
        gather_start(i + 1)

    changed = jnp.logical_or(i == 0, te_ref[i] != te_ref[jnp.maximum(i - 1, 0)])

    @pl.when(jnp.logical_and(live, changed))
    def _():
        wg_b[...] = wg_ref[0, 0].astype(BF16)
        wu_b[...] = wu_ref[0, 0].astype(BF16)
        wd_b[...] = wd_ref[0, 0].astype(BF16)

    @pl.when(live)
    def _():
        gather_wait(i % 2)
        x = jnp.concatenate([c.astype(BF16) for c in _slab_load(xbuf.at[i % 2], tm)],
                            axis=1)
        gate = jnp.dot(x, wg_b[...], preferred_element_type=F32)
        up = jnp.dot(x, wu_b[...], preferred_element_type=F32)
        act = (_silu(gate) * up).astype(BF16)
        y = jnp.dot(act, wd_b[...], preferred_element_type=F32)

        @pl.when(i > 0)
        def _():
            scatter_wait()

        _slab_store(ybuf, y)
        scatter_start(i)

    @pl.when(i == n_tiles - 1)
    def _():
        scatter_wait()


def _experts(hn, src, dst, tile_expert, n_live, w_gate, w_up, w_down, layer, n_dst):
    d = D_MODEL
    n_slots = src.shape[0]
    f = w_gate.shape[3]

    def wsel(i, src_r, dst_r, te, nv):
        return (layer, te[i], 0, 0)

    return pl.pallas_call(
        functools.partial(_expert_kernel, n_dst=n_dst),
        grid_spec=pltpu.PrefetchScalarGridSpec(
            num_scalar_prefetch=4,
            grid=(n_slots // EXPERT_TM,),
            in_specs=[
                pl.BlockSpec(memory_space=pl.ANY),
                pl.BlockSpec((1, 1, d, f), wsel),
                pl.BlockSpec((1, 1, d, f), wsel),
                pl.BlockSpec((1, 1, f, d), wsel),
            ],
            out_specs=pl.BlockSpec(memory_space=pl.ANY),
            scratch_shapes=[pltpu.VMEM((2, EXPERT_TM * SLAB, V7X_LANES), U32),
                            pltpu.VMEM((EXPERT_TM * SLAB, V7X_LANES), U32),
                            pltpu.VMEM((d, f), BF16), pltpu.VMEM((d, f), BF16),
                            pltpu.VMEM((f, d), BF16),
                            pltpu.SemaphoreType.DMA((2,)), pltpu.SemaphoreType.DMA(())],
        ),
        out_shape=jax.ShapeDtypeStruct((n_dst * SLAB, V7X_LANES), U32),
        compiler_params=pltpu.CompilerParams(dimension_semantics=("arbitrary",),
                                             vmem_limit_bytes=52 * 1024 * 1024,
                                             has_side_effects=True),
        name="moe_experts",
    )(src, dst, tile_expert, n_live, hn, w_gate, w_up, w_down)


def _combine_kernel(h_ref, w1_ref, w2_ref, fw_ref, y1_ref, y2_ref, o_ref, *, final_norm):
    tm = h_ref.shape[0]
    w1 = w1_ref[...]
    w2 = w2_ref[...]
    ssq = jnp.zeros((tm, 1), F32)
    y1 = _slab_load(y1_ref, tm)
    y2 = _slab_load(y2_ref, tm)
    for j in range(2 * SLAB):
        cols = slice(j * V7X_LANES, (j + 1) * V7X_LANES)
        out = h_ref[:, cols] + w1 * y1[j] + w2 * y2[j]
        o_ref[:, cols] = out
        ssq = ssq + jnp.sum(out * out, axis=-1, keepdims=True)
    if final_norm:
        inv = lax.rsqrt(ssq / D_MODEL + EPS)
        o_ref[...] = (o_ref[...] * inv) * fw_ref[...]


def _combine(h, ys, w1, w2, final_w, final_norm):
    t, d = h.shape
    tm = min(COMBINE_TM, t)
    nt = t // tm
    return pl.pallas_call(
        functools.partial(_combine_kernel, final_norm=final_norm),
        grid=(nt,),
        in_specs=[
            pl.BlockSpec((tm, d), lambda i: (i, 0)),
            pl.BlockSpec((tm, 1), lambda i: (i, 0)),
            pl.BlockSpec((tm, 1), lambda i: (i, 0)),
            pl.BlockSpec((1, d), lambda i: (0, 0)),
            pl.BlockSpec((tm * SLAB, V7X_LANES), lambda i: (i, 0)),
            pl.BlockSpec((tm * SLAB, V7X_LANES), lambda i: (i + nt, 0)),
        ],
        out_specs=pl.BlockSpec((tm, d), lambda i: (i, 0)),
        out_shape=jax.ShapeDtypeStruct((t, d), F32),
        compiler_params=_cparams(1, 32),
        name="moe_combine",
    )(h, w1.reshape(t, 1), w2.reshape(t, 1), final_w.reshape(1, d), ys, ys)


def _moe(h, nw, w_group, b_group, w_router, b_router, w_gate, w_up, w_down, layer, final_w,
         final_norm):
    t, d = h.shape
    hn, idx, gw, cnt = _route(h, nw, w_group, b_group, w_router, b_router)
    counts = cnt[:, 0]
    padded = ((counts + EXPERT_TM - 1) // EXPERT_TM) * EXPERT_TM
    seg_end = jnp.cumsum(padded)
    seg_start = seg_end - padded
    pos = jnp.concatenate([seg_start[idx[0]] + idx[2], seg_start[idx[1]] + idx[3]]).astype(I32)
    n_tiles = (2 * t) // EXPERT_TM + N_EXPERTS
    n_slots = n_tiles * EXPERT_TM
    n_live = (seg_end[-1] // EXPERT_TM).astype(I32)
    tile_first = jnp.minimum(jnp.arange(n_tiles, dtype=I32), n_live - 1) * EXPERT_TM
    tile_expert = jnp.sum(tile_first[:, None] >= seg_end[None, :], axis=1).astype(I32)
    n_dst = 2 * t + EXPERT_TM
    spare = 2 * t + jnp.arange(n_slots, dtype=I32) % EXPERT_TM
    dst = spare.at[pos].set(jnp.arange(2 * t, dtype=I32), unique_indices=True,
                            mode="promise_in_bounds")
    src = jnp.where(dst < 2 * t, dst % t, 0)
    ys = _experts(hn, src, dst, tile_expert, n_live.reshape(1), w_gate, w_up, w_down, layer, n_dst)
    return _combine(h, ys, gw[0], gw[1], final_w, final_norm)


def kernel(x, norm_mix_w, norm_ffn_w, final_norm_w, mix_w_in, pool_w, pool_scale, attn_sinks,
           mix_w_out, ssm_w_in, ssm_conv_w, ssm_conv_b, ssm_dt_bias, ssm_a_log, ssm_d,
           ssm_norm_w, ssm_w_out, moe_w_group, moe_b_group, moe_w_router, moe_b_router,
           moe_w_gate, moe_w_up, moe_w_down):
    batch, seq_len, d = x.shape
    h = x.reshape(batch * seq_len, d)

    proj = _norm_matmul(h, norm_mix_w[0], mix_w_in, MIX_IN_WIDTH, MIX_IN_WIDTH // 5, BF16)
    mixed = _pool_attn(proj, pool_w, pool_scale, attn_sinks, seq_len)
    h = _matmul_residual(mixed, mix_w_out, h, D_MODEL // 2)
    h = _moe(h, norm_ffn_w[0], moe_w_group[0], moe_b_group[0], moe_w_router[0], moe_b_router[0],
             moe_w_gate, moe_w_up, moe_w_down, 0, final_norm_w, False)

    ssm_w_in_t = jnp.swapaxes(ssm_w_in, 1, 2)
    zx = _norm_matmul(h, norm_mix_w[1], ssm_w_in_t, ZX_WIDTH, ZX_WIDTH // 10, BF16, True)
    dt_raw = _norm_matmul(h, norm_mix_w[1], ssm_w_in_t[:, ZX_WIDTH:, :], N_SSM_HEADS, N_SSM_HEADS,
                          F32, True)
    y = _ssd(zx, dt_raw, ssm_conv_w, ssm_conv_b, ssm_dt_bias, ssm_a_log, ssm_d, ssm_norm_w,
             batch, seq_len)
    h = _matmul_residual(y, ssm_w_out, h, D_MODEL // 4)
    h = _moe(h, norm_ffn_w[1], moe_w_group[1], moe_b_group[1], moe_w_router[1], moe_b_router[1],
             moe_w_gate, moe_w_up, moe_w_down, 1, final_norm_w, True)
    return h.reshape(batch, seq_len, d)
```

```python
import functools

import jax
import jax.numpy as jnp
from jax import lax
from jax.experimental import pallas as pl
from jax.experimental.pallas import tpu as pltpu

F32 = jnp.float32
BF16 = jnp.bfloat16
I32 = jnp.int32

D_MODEL = 2048
EPS = 1e-6
POOL_WINDOWS = (2, 4, 8, 16)
POOL_WIDTH = 1024
POOL_GROUP = 256
HEAD_DIM = 64
N_Q_HEADS = 16
N_KV_HEADS = 4
Q_PER_KV = 4
ATTN_WIDTH = 1024
KV_WIDTH = 256
WINDOW = 128
BLOCK = 128
MIX_IN_WIDTH = 2560
D_INNER = 4096
SSM_HEAD_DIM = 64
N_SSM_HEADS = 64
N_SSM_GROUPS = 8
HEADS_PER_GROUP = 8
D_STATE = 128
CONV_WIDTH = 4
CHUNK = 128
CONV_CH = D_INNER + 2 * N_SSM_GROUPS * D_STATE
ZX_WIDTH = D_INNER + CONV_CH
N_EXPERT_GROUPS = 4
EXPERTS_PER_GROUP = 4
N_EXPERTS = 16
D_FF_EXPERT = 512
SLOPES = tuple(float(2.0 ** (-8.0 * (i + 1) / N_Q_HEADS)) for i in range(N_Q_HEADS))

V7X_LANES = 128
V7X_SUBLANES = 8

PROJ_TM = 1024
NORM_ROWS = 128
ROUTE_TM = 512
EXPERT_TM = 256
COMBINE_TM = 256
ROUTE_ROWS = 32
EXPERT_ROW0 = 8
DMA_UNROLL = 32
GATHER_SLOTS = 3
HALF = D_MODEL // 2
SLAB = HALF // V7X_LANES


def _cparams(n_axes, vmem_mb):
    return pltpu.CompilerParams(
        dimension_semantics=("arbitrary",) * n_axes,
        vmem_limit_bytes=vmem_mb * 1024 * 1024,
    )


def _silu(x):
    half = 0.5 * x
    return half + half * jnp.tanh(half)


def _split_bf16(x):
    hi = x.astype(BF16)
    lo = (x - hi.astype(F32)).astype(BF16)
    return hi, lo


def _slab_rows(j, n):
    return pl.ds(j, n, stride=SLAB)


U32 = jnp.uint32


def _pack_bf16_pair(lo, hi):
    lo_bits = lax.bitcast_convert_type(lo.astype(BF16).astype(F32), U32)
    hi_bits = lax.bitcast_convert_type(hi.astype(BF16).astype(F32), U32)
    return (lo_bits >> 16) | hi_bits


def _unpack_bf16_pair(word):
    lo = lax.bitcast_convert_type(word << 16, F32)
    hi = lax.bitcast_convert_type(word & jnp.uint32(0xFFFF0000), F32)
    return lo, hi


def _slab_store(ref, x):
    n = x.shape[0]
    for j in range(SLAB):
        lo = x[:, j * V7X_LANES:(j + 1) * V7X_LANES]
        hi = x[:, HALF + j * V7X_LANES:HALF + (j + 1) * V7X_LANES]
        ref[_slab_rows(j, n), :] = _pack_bf16_pair(lo, hi)


def _slab_load(ref, n):
    pairs = [_unpack_bf16_pair(ref[_slab_rows(j, n), :]) for j in range(SLAB)]
    return [p[0] for p in pairs] + [p[1] for p in pairs]


def _norm_matmul_kernel(x_ref, nw_ref, w_ref, o_ref, xn_ref, *, w_transposed):
    tm = x_ref.shape[0]

    @pl.when(pl.program_id(1) == 0)
    def _():
        def body(r, carry):
            rows = pl.ds(pl.multiple_of(r * NORM_ROWS, NORM_ROWS), NORM_ROWS)
            x = x_ref[rows, :]
            var = jnp.mean(x * x, axis=-1, keepdims=True)
            xn_ref[rows, :] = ((x * lax.rsqrt(var + EPS)) * nw_ref[...]).astype(BF16)
            return carry

        lax.fori_loop(0, tm // NORM_ROWS, body, 0)

    contract = ((1,), (1,)) if w_transposed else ((1,), (0,))
    o_ref[...] = lax.dot_general(xn_ref[...], w_ref[0].astype(BF16), (contract, ((), ())),
                                 preferred_element_type=F32).astype(o_ref.dtype)


def _norm_matmul(x, nw, w, n_out, tn, out_dtype, w_transposed=False):
    t, d = x.shape
    tm = min(PROJ_TM, t)
    if w_transposed:
        w_spec = pl.BlockSpec((1, tn, d), lambda i, j: (0, j, 0))
    else:
        w_spec = pl.BlockSpec((1, d, tn), lambda i, j: (0, 0, j))
    return pl.pallas_call(
        functools.partial(_norm_matmul_kernel, w_transposed=w_transposed),
        grid=(t // tm, n_out // tn),
        in_specs=[
            pl.BlockSpec((tm, d), lambda i, j: (i, 0)),
            pl.BlockSpec((1, d), lambda i, j: (0, 0)),
            w_spec,
        ],
        out_specs=pl.BlockSpec((tm, tn), lambda i, j: (i, j)),
        out_shape=jax.ShapeDtypeStruct((t, n_out), out_dtype),
        scratch_shapes=[pltpu.VMEM((tm, d), BF16)],
        compiler_params=_cparams(2, 56),
        name="norm_matmul",
    )(x, nw.reshape(1, d), w)


def _matmul_res_kernel(a_ref, w_ref, r_ref, o_ref):
    o_ref[...] = r_ref[...] + jnp.dot(a_ref[...], w_ref[0].astype(BF16),
                                      preferred_element_type=F32)


def _matmul_residual(a, w, res, tn):
    t, k = a.shape
    n = w.shape[2]
    tm = min(PROJ_TM, t)
    return pl.pallas_call(
        _matmul_res_kernel,
        grid=(t // tm, n // tn),
        in_specs=[
            pl.BlockSpec((tm, k), lambda i, j: (i, 0)),
            pl.BlockSpec((1, k, tn), lambda i, j: (0, 0, j)),
            pl.BlockSpec((tm, tn), lambda i, j: (i, j)),
        ],
        out_specs=pl.BlockSpec((tm, tn), lambda i, j: (i, j)),
        out_shape=jax.ShapeDtypeStruct((t, n), F32),
        compiler_params=_cparams(2, 56),
        name="matmul_residual",
    )(a, w, res)


def _pool_attn_kernel(sinks_ref, u_ref, up_ref, q_ref, k_ref, kp_ref, v_ref, vp_ref,
                      pw_ref, ps_ref, o_ref, *, blocks_per_seq):
    blk = pl.program_id(0) % blocks_per_seq
    first = blk == 0
    row = lax.broadcasted_iota(I32, (BLOCK, 2 * BLOCK), 0)
    col = lax.broadcasted_iota(I32, (BLOCK, 2 * BLOCK), 1)
    dist = row + BLOCK - col

    u_cur = u_ref[...]
    u_prev = jnp.where(first, jnp.zeros_like(u_cur), up_ref[...])
    u_ext = jnp.concatenate([u_prev, u_cur], axis=0)
    pos = blk * BLOCK + lax.broadcasted_iota(I32, (BLOCK, 1), 0)
    for g, w in enumerate(POOL_WINDOWS):
        cols = slice(g * POOL_GROUP, (g + 1) * POOL_GROUP)
        band = jnp.where((dist >= 0) & (dist < w), 1.0, 0.0).astype(BF16)
        wsum = jnp.dot(band, u_ext[:, cols], preferred_element_type=F32)
        count = jnp.minimum(pos + 1, w).astype(F32)
        pooled = (wsum / count - u_cur[:, cols].astype(F32)).astype(BF16)
        out_g = jnp.dot(pooled, pw_ref[0, g].astype(BF16), preferred_element_type=F32)
        o_ref[:, cols] = (out_g * ps_ref[:, cols]).astype(o_ref.dtype)

    kk = jnp.concatenate([kp_ref[...], k_ref[...]], axis=0)
    vv = jnp.concatenate([vp_ref[...], v_ref[...]], axis=0)
    valid = (dist >= 0) & (dist < WINDOW) & ((col >= BLOCK) | jnp.logical_not(first))
    dist_f = dist.astype(F32)
    scale = HEAD_DIM ** -0.5
    for pair in range(N_Q_HEADS // 2):
        q_pair = q_ref[:, pair * 2 * HEAD_DIM:(pair + 1) * 2 * HEAD_DIM]
        outs = []
        for sub in range(2):
            h = pair * 2 + sub
            g = h // Q_PER_KV
            qh = q_pair[:, sub * HEAD_DIM:(sub + 1) * HEAD_DIM]
            kg = kk[:, g * HEAD_DIM:(g + 1) * HEAD_DIM]
            vg = vv[:, g * HEAD_DIM:(g + 1) * HEAD_DIM]
            s = lax.dot_general(qh, kg, (((1,), (1,)), ((), ())),
                                preferred_element_type=F32) * scale
            s = jnp.where(valid, s - SLOPES[h] * dist_f, -jnp.inf)
            sink = sinks_ref[0, h]
            m = jnp.maximum(jnp.max(s, axis=-1, keepdims=True), sink)
            p = jnp.exp(s - m)
            denom = jnp.sum(p, axis=-1, keepdims=True) + jnp.exp(sink - m)
            probs = (p / denom).astype(BF16)
            outs.append(jnp.dot(probs, vg, preferred_element_type=F32))
        lo = ATTN_WIDTH + pair * 2 * HEAD_DIM
        o_ref[:, lo:lo + 2 * HEAD_DIM] = jnp.concatenate(outs, axis=1).astype(o_ref.dtype)


def _pool_attn(proj, pool_w, pool_scale, sinks, seq_len):
    t = proj.shape[0]
    nblk = t // BLOCK
    kcol = (POOL_WIDTH + ATTN_WIDTH) // KV_WIDTH

    def cur(cb):
        return lambda i: (i, cb)

    def prev(cb):
        return lambda i: (jnp.maximum(i - 1, 0), cb)

    return pl.pallas_call(
        functools.partial(_pool_attn_kernel, blocks_per_seq=seq_len // BLOCK),
        grid=(nblk,),
        in_specs=[
            pl.BlockSpec(memory_space=pltpu.SMEM),
            pl.BlockSpec((BLOCK, POOL_WIDTH), cur(0)),
            pl.BlockSpec((BLOCK, POOL_WIDTH), prev(0)),
            pl.BlockSpec((BLOCK, ATTN_WIDTH), cur(1)),
            pl.BlockSpec((BLOCK, KV_WIDTH), cur(kcol)),
            pl.BlockSpec((BLOCK, KV_WIDTH), prev(kcol)),
            pl.BlockSpec((BLOCK, KV_WIDTH), cur(kcol + 1)),
            pl.BlockSpec((BLOCK, KV_WIDTH), prev(kcol + 1)),
            pl.BlockSpec((1, len(POOL_WINDOWS), POOL_GROUP, POOL_GROUP), lambda i: (0, 0, 0, 0)),
            pl.BlockSpec((1, POOL_WIDTH), lambda i: (0, 0)),
        ],
        out_specs=pl.BlockSpec((BLOCK, POOL_WIDTH + ATTN_WIDTH), lambda i: (i, 0)),
        out_shape=jax.ShapeDtypeStruct((t, POOL_WIDTH + ATTN_WIDTH), BF16),
        compiler_params=_cparams(1, 32),
        name="pool_attn",
    )(sinks, proj, proj, proj, proj, proj, proj, proj, pool_w, pool_scale)


def _ssd_kernel(z_ref, x_ref, bc_ref, dt_ref, cw_ref, cb_ref, dtb_ref, alog_ref, dexp_ref,
                nw_ref, e2_ref, o_ref, prev_ref, state_ref, y_ref):
    gn = N_SSM_GROUPS * D_STATE
    taps = CONV_WIDTH - 1

    @pl.when(pl.program_id(1) == 0)
    def _():
        prev_ref[...] = jnp.zeros_like(prev_ref)
        state_ref[...] = jnp.zeros_like(state_ref)

    cur = jnp.concatenate([x_ref[...], bc_ref[...]], axis=1)
    cat = jnp.concatenate([prev_ref[...], cur], axis=0)
    prev_ref[...] = cur
    srow = lax.broadcasted_iota(I32, (taps * CHUNK, 2 * CHUNK), 0)
    scol = lax.broadcasted_iota(I32, (taps * CHUNK, 2 * CHUNK), 1)
    shift = jnp.where(scol + taps - srow // CHUNK == srow % CHUNK + CHUNK, 1.0, 0.0).astype(BF16)
    shifted = jnp.dot(shift, cat, preferred_element_type=F32)
    conv = cb_ref[...] + cw_ref[taps:taps + 1, :] * cur.astype(F32)
    for j in range(taps):
        conv = conv + cw_ref[j:j + 1, :] * shifted[j * CHUNK:(j + 1) * CHUNK, :]
    act = _silu(conv)
    xs = act[:, 0:D_INNER]
    xs_b = xs.astype(BF16)
    bm = act[:, D_INNER:D_INNER + gn]
    cm = act[:, D_INNER + gn:CONV_CH]

    dt_in = dt_ref[...] + dtb_ref[...]
    dt = jnp.maximum(dt_in, 0.0) + jnp.log(1.0 + jnp.exp(-jnp.abs(dt_in)))
    da = dt * (-jnp.exp(alog_ref[...]))
    row = lax.broadcasted_iota(I32, (CHUNK, CHUNK), 0)
    col = lax.broadcasted_iota(I32, (CHUNK, CHUNK), 1)
    causal = row >= col
    tri = jnp.where(causal, 1.0, 0.0).astype(BF16)
    da_hi, da_lo = _split_bf16(da)
    a_cum = (jnp.dot(tri, da_hi, preferred_element_type=F32)
             + jnp.dot(tri, da_lo, preferred_element_type=F32))
    e_cum = jnp.exp(a_cum)
    tr = jnp.concatenate([a_cum, dt], axis=1).T
    a_cum_t = tr[0:N_SSM_HEADS, :]
    dt_t = tr[N_SSM_HEADS:2 * N_SSM_HEADS, :]
    a_last_col = a_cum_t[:, CHUNK - 1:CHUNK]
    w_t = dt_t * jnp.exp(a_last_col - a_cum_t)
    dec8 = jnp.broadcast_to(e_cum[CHUNK - 1:CHUNK, :], (V7X_SUBLANES, N_SSM_HEADS))
    dec_hi, dec_lo = _split_bf16(dec8)
    dec_exp = (jnp.dot(dec_hi, e2_ref[...], preferred_element_type=F32)
               + jnp.dot(dec_lo, e2_ref[...], preferred_element_type=F32))[0:1, :]

    lane = lax.broadcasted_iota(I32, (CHUNK, 2 * SSM_HEAD_DIM), 1)
    left = lane < SSM_HEAD_DIM
    for g in range(N_SSM_GROUPS):
        b_g = bm[:, g * D_STATE:(g + 1) * D_STATE]
        c_g = cm[:, g * D_STATE:(g + 1) * D_STATE]
        bt_g = b_g.T
        cb_g = lax.dot_general(c_g.astype(BF16), b_g.astype(BF16),
                               (((1,), (1,)), ((), ())), preferred_element_type=F32)
        for k in range(HEADS_PER_GROUP // 2):
            slab = slice((g * 4 + k) * 2 * SSM_HEAD_DIM, (g * 4 + k + 1) * 2 * SSM_HEAD_DIM)
            lhs_y = []
            lhs_s = []
            for sub in range(2):
                h = g * HEADS_PER_GROUP + 2 * k + sub
                colb = jnp.broadcast_to(a_cum[:, h:h + 1], (CHUNK, CHUNK))
                rowb = jnp.broadcast_to(a_cum_t[h:h + 1, :], (CHUNK, CHUNK))
                decay = jnp.exp(jnp.where(causal, colb - rowb, -jnp.inf))
                m_h = cb_g * decay * jnp.broadcast_to(dt_t[h:h + 1, :], (CHUNK, CHUNK))
                lhs_y.append(m_h.astype(BF16))
                lhs_y.append((c_g * jnp.broadcast_to(e_cum[:, h:h + 1], (CHUNK, CHUNK))).astype(BF16))
                lhs_s.append((bt_g * jnp.broadcast_to(w_t[h:h + 1, :], (CHUNK, CHUNK))).astype(BF16))
            x_slab = xs_b[:, slab]
            s_slab = state_ref[:, slab]
            s_slab_b = s_slab.astype(BF16)
            zero = jnp.zeros_like(x_slab)
            x_l = jnp.where(left, x_slab, zero)
            x_r = jnp.where(left, zero, x_slab)
            s_l = jnp.where(left, s_slab_b, zero)
            s_r = jnp.where(left, zero, s_slab_b)
            y_pair = jnp.dot(jnp.concatenate(lhs_y, axis=1),
                             jnp.concatenate([x_l, s_l, x_r, s_r], axis=0),
                             preferred_element_type=F32)
            y_ref[:, slab] = y_pair
            d_state = jnp.dot(jnp.concatenate(lhs_s, axis=1),
                              jnp.concatenate([x_l, x_r], axis=0),
                              preferred_element_type=F32)
            state_ref[:, slab] = s_slab * dec_exp[:, slab] + d_state

    gsz = D_INNER // N_SSM_GROUPS
    for g in range(N_SSM_GROUPS):
        cols = slice(g * gsz, (g + 1) * gsz)
        y = y_ref[:, cols] + dexp_ref[:, cols] * xs[:, cols]
        gated = y * _silu(z_ref[:, cols].astype(F32))
        var = jnp.mean(gated * gated, axis=-1, keepdims=True)
        o_ref[:, cols] = ((gated * lax.rsqrt(var + EPS)) * nw_ref[:, cols]).astype(o_ref.dtype)


def _ssd(zx, dt_raw, conv_w, conv_b, dt_bias, a_log, d_skip, norm_w, batch, seq_len):
    t = zx.shape[0]
    nc = seq_len // CHUNK
    d_exp = jnp.repeat(d_skip, SSM_HEAD_DIM, axis=1)
    e2 = jnp.repeat(jnp.eye(N_SSM_HEADS, dtype=BF16), SSM_HEAD_DIM, axis=1)
    bc_col = (2 * D_INNER) // (2 * N_SSM_GROUPS * D_STATE)

    def rows(cb):
        return lambda b, c: (b * nc + c, cb)

    def whole(b, c):
        return (0, 0)

    return pl.pallas_call(
        _ssd_kernel,
        grid=(batch, nc),
        in_specs=[
            pl.BlockSpec((CHUNK, D_INNER), rows(0)),
            pl.BlockSpec((CHUNK, D_INNER), rows(1)),
            pl.BlockSpec((CHUNK, 2 * N_SSM_GROUPS * D_STATE), rows(bc_col)),
            pl.BlockSpec((CHUNK, N_SSM_HEADS), rows(0)),
            pl.BlockSpec((None, CONV_WIDTH, CONV_CH), lambda b, c: (0, 0, 0)),
            pl.BlockSpec((1, CONV_CH), whole),
            pl.BlockSpec((1, N_SSM_HEADS), whole),
            pl.BlockSpec((1, N_SSM_HEADS), whole),
            pl.BlockSpec((1, D_INNER), whole),
            pl.BlockSpec((1, D_INNER), whole),
            pl.BlockSpec((N_SSM_HEADS, D_INNER), whole),
        ],
        out_specs=pl.BlockSpec((CHUNK, D_INNER), rows(0)),
        out_shape=jax.ShapeDtypeStruct((t, D_INNER), BF16),
        scratch_shapes=[
            pltpu.VMEM((CHUNK, CONV_CH), BF16),
            pltpu.VMEM((D_STATE, D_INNER), F32),
            pltpu.VMEM((CHUNK, D_INNER), F32),
        ],
        compiler_params=_cparams(2, 48),
        name="ssd",
    )(zx, zx, zx, dt_raw, conv_w, conv_b, dt_bias, a_log, d_exp, norm_w, e2)


def _route_kernel(h_ref, nw_ref, wt_ref, bias_ref, hn_ref, idx_ref, gw_ref, cnt_ref, carry_ref):
    tm = h_ref.shape[0]

    @pl.when(pl.program_id(0) == 0)
    def _():
        carry_ref[...] = jnp.zeros_like(carry_ref)

    h = h_ref[...]
    var = jnp.mean(h * h, axis=-1, keepdims=True)
    hn = (h * lax.rsqrt(var + EPS)) * nw_ref[...]
    _slab_store(hn_ref, hn)

    nt = (((1,), (1,)), ((), ()))
    hn_hi, hn_lo = _split_bf16(hn)
    w_hi, w_lo = _split_bf16(wt_ref[...])
    logits = (lax.dot_general(w_hi, hn_hi, nt, preferred_element_type=F32)
              + lax.dot_general(w_hi, hn_lo, nt, preferred_element_type=F32)
              + lax.dot_general(w_lo, hn_hi, nt, preferred_element_type=F32)
              + bias_ref[...])

    def lrow(r):
        return logits[r:r + 1, :]

    gl = [lrow(g) for g in range(N_EXPERT_GROUPS)]
    gmax = functools.reduce(jnp.maximum, gl)
    gidx = jnp.full((1, tm), N_EXPERT_GROUPS - 1, I32)
    for g in range(N_EXPERT_GROUPS - 2, -1, -1):
        gidx = jnp.where(gl[g] == gmax, g, gidx)
    gsum = functools.reduce(jnp.add, [jnp.exp(x - gmax) for x in gl])
    g_weight = 1.0 / gsum

    el = []
    for e in range(EXPERTS_PER_GROUP):
        v = lrow(EXPERT_ROW0 + e)
        for g in range(1, N_EXPERT_GROUPS):
            v = jnp.where(gidx == g, lrow(EXPERT_ROW0 + g * EXPERTS_PER_GROUP + e), v)
        el.append(v)

    def first_argmax(vals):
        m = functools.reduce(jnp.maximum, vals)
        idx = jnp.full((1, tm), len(vals) - 1, I32)
        for e in range(len(vals) - 2, -1, -1):
            idx = jnp.where(vals[e] == m, e, idx)
        return m, idx

    m1, i1 = first_argmax(el)
    el2 = [jnp.where(i1 == e, -jnp.inf, el[e]) for e in range(EXPERTS_PER_GROUP)]
    m2, i2 = first_argmax(el2)
    e21 = jnp.exp(m2 - m1)
    w1 = g_weight / (1.0 + e21)
    w2 = g_weight * e21 / (1.0 + e21)
    eid1 = gidx * EXPERTS_PER_GROUP + i1
    eid2 = gidx * EXPERTS_PER_GROUP + i2

    erow = lax.broadcasted_iota(I32, (N_EXPERTS, tm), 0)
    oh1 = erow == eid1
    oh2 = erow == eid2
    member = jnp.where(oh1 | oh2, 1.0, 0.0)
    s_i = lax.broadcasted_iota(I32, (tm, tm), 0)
    t_i = lax.broadcasted_iota(I32, (tm, tm), 1)
    before = jnp.where(s_i < t_i, 1.0, 0.0).astype(BF16)
    excl = jnp.dot(member.astype(BF16), before, preferred_element_type=F32)
    carry = carry_ref[...]
    tot = excl + carry[:, 0:1]
    rank1 = jnp.sum(jnp.where(oh1, tot, 0.0), axis=0, keepdims=True)
    rank2 = jnp.sum(jnp.where(oh2, tot, 0.0), axis=0, keepdims=True)
    carry = carry + jnp.sum(member, axis=1, keepdims=True)
    carry_ref[...] = carry
    cnt_ref[...] = carry.astype(I32)

    idx_ref[...] = jnp.zeros_like(idx_ref)
    idx_ref[0:1, :] = eid1
    idx_ref[1:2, :] = eid2
    idx_ref[2:3, :] = rank1.astype(I32)
    idx_ref[3:4, :] = rank2.astype(I32)
    gw_ref[...] = jnp.zeros_like(gw_ref)
    gw_ref[0:1, :] = w1
    gw_ref[1:2, :] = w2


def _route(h, nw, w_group, b_group, w_router, b_router):
    t, d = h.shape
    tm = min(ROUTE_TM, t)
    wt = jnp.zeros((ROUTE_ROWS, d), F32)
    wt = wt.at[0:N_EXPERT_GROUPS].set(w_group.T)
    wt = wt.at[EXPERT_ROW0:EXPERT_ROW0 + N_EXPERTS].set(
        jnp.transpose(w_router, (0, 2, 1)).reshape(N_EXPERTS, d))
    bias = jnp.zeros((ROUTE_ROWS, 1), F32)
    bias = bias.at[0:N_EXPERT_GROUPS, 0].set(b_group)
    bias = bias.at[EXPERT_ROW0:EXPERT_ROW0 + N_EXPERTS, 0].set(b_router.reshape(N_EXPERTS))
    return pl.pallas_call(
        _route_kernel,
        grid=(t // tm,),
        in_specs=[
            pl.BlockSpec((tm, d), lambda i: (i, 0)),
            pl.BlockSpec((1, d), lambda i: (0, 0)),
            pl.BlockSpec((ROUTE_ROWS, d), lambda i: (0, 0)),
            pl.BlockSpec((ROUTE_ROWS, 1), lambda i: (0, 0)),
        ],
        out_specs=[
            pl.BlockSpec((tm * SLAB, V7X_LANES), lambda i: (i, 0)),
            pl.BlockSpec((V7X_SUBLANES, tm), lambda i: (0, i)),
            pl.BlockSpec((V7X_SUBLANES, tm), lambda i: (0, i)),
            pl.BlockSpec((N_EXPERTS, V7X_LANES), lambda i: (0, 0)),
        ],
        out_shape=[
            jax.ShapeDtypeStruct((t * SLAB, V7X_LANES), U32),
            jax.ShapeDtypeStruct((V7X_SUBLANES, t), I32),
            jax.ShapeDtypeStruct((V7X_SUBLANES, t), F32),
            jax.ShapeDtypeStruct((N_EXPERTS, V7X_LANES), I32),
        ],
        scratch_shapes=[pltpu.VMEM((N_EXPERTS, V7X_LANES), F32)],
        compiler_params=_cparams(1, 40),
        name="moe_route",
    )(h, nw.reshape(1, d), wt, bias)


def _expert_kernel(src_ref, dst_ref, te_ref, nv_ref, hn_hbm, wg_ref, wu_ref, wd_ref, out_hbm,
                   xbuf, ybuf, wg_b, wu_b, wd_b, gsems, ssem, *, n_dst):
    i = pl.program_id(0)
    n_tiles = pl.num_programs(0)
    n_live = nv_ref[0]
    live = i < n_live
    tm = EXPERT_TM

    def slab_at(r):
        return pl.ds(pl.multiple_of(r * SLAB, SLAB), SLAB)

    def gather_copy(tile_slot, r, tok):
        return pltpu.make_async_copy(hn_hbm.at[slab_at(tok)], xbuf.at[tile_slot, slab_at(r)],
                                     gsems.at[tile_slot])

    def scatter_copy(r, dst_row):
        return pltpu.make_async_copy(ybuf.at[slab_at(r)], out_hbm.at[slab_at(dst_row)], ssem)

    def issue(gather_tile, scatter_tile):
        def body(rr, carry):
            for u in range(DMA_UNROLL):
                r = rr * DMA_UNROLL + u
                if scatter_tile is not None:
                    scatter_copy(r, dst_ref[scatter_tile * tm + r]).start(priority=u % 2)
                if gather_tile is not None:
                    gather_copy(gather_tile % GATHER_SLOTS, r,
                                src_ref[gather_tile * tm + r]).start(priority=u % 2)
            return carry

        lax.fori_loop(0, tm // DMA_UNROLL, body, 0)

    def gather_wait(tile_slot):
        def body(r, carry):
            gather_copy(tile_slot, 0, 0).wait()
            return carry

        lax.fori_loop(0, tm, body, 0, unroll=DMA_UNROLL)

    def scatter_wait():
        def body(r, carry):
            scatter_copy(0, 0).wait()
            return carry

        lax.fori_loop(0, tm, body, 0, unroll=DMA_UNROLL)

    def dump_copy():
        return pltpu.make_async_copy(ybuf, out_hbm.at[pl.ds((n_dst - tm) * SLAB, tm * SLAB)], ssem)

    @pl.when(i == 0)
    def _():
        ybuf[...] = jnp.zeros_like(ybuf)
        dump_copy().start()
        issue(i, None)
        dump_copy().wait()

    @pl.when(jnp.logical_and(i == 0, 1 < n_live))
    def _():
        issue(i + 1, None)

    changed = jnp.logical_or(i == 0, te_ref[i] != te_ref[jnp.maximum(i - 1, 0)])

    @pl.when(jnp.logical_and(live, changed))
    def _():
        wg_b[...] = wg_ref[0, 0].astype(BF16)
        wu_b[...] = wu_ref[0, 0].astype(BF16)
        wd_b[...] = wd_ref[0, 0].astype(BF16)

    @pl.when(live)
    def _():
        gather_wait(i % GATHER_SLOTS)
        x = jnp.concatenate([c.astype(BF16) for c in _slab_load(xbuf.at[i % GATHER_SLOTS], tm)],
                            axis=1)
        gate = jnp.dot(x, wg_b[...], preferred_element_type=F32)
        up = jnp.dot(x, wu_b[...], preferred_element_type=F32)
        act = (_silu(gate) * up).astype(BF16)
        y = jnp.dot(act, wd_b[...], preferred_element_type=F32)

        @pl.when(i > 0)
        def _():
            scatter_wait()

        _slab_store(ybuf, y)

    ahead = i + GATHER_SLOTS - 1

    @pl.when(jnp.logical_and(live, ahead < n_live))
    def _():
        issue(ahead, i)

    @pl.when(jnp.logical_and(live, ahead >= n_live))
    def _():
        issue(None, i)

    @pl.when(i == n_tiles - 1)
    def _():
        scatter_wait()


def _experts(hn, src, dst, tile_expert, n_live, w_gate, w_up, w_down, layer, n_dst):
    d = D_MODEL
    n_slots = src.shape[0]
    f = w_gate.shape[3]

    def wsel(i, src_r, dst_r, te, nv):
        return (layer, te[i], 0, 0)

    return pl.pallas_call(
        functools.partial(_expert_kernel, n_dst=n_dst),
        grid_spec=pltpu.PrefetchScalarGridSpec(
            num_scalar_prefetch=4,
            grid=(n_slots // EXPERT_TM,),
            in_specs=[
                pl.BlockSpec(memory_space=pl.ANY),
                pl.BlockSpec((1, 1, d, f), wsel),
                pl.BlockSpec((1, 1, d, f), wsel),
                pl.BlockSpec((1, 1, f, d), wsel),
            ],
            out_specs=pl.BlockSpec(memory_space=pl.ANY),
            scratch_shapes=[pltpu.VMEM((GATHER_SLOTS, EXPERT_TM * SLAB, V7X_LANES), U32),
                            pltpu.VMEM((EXPERT_TM * SLAB, V7X_LANES), U32),
                            pltpu.VMEM((d, f), BF16), pltpu.VMEM((d, f), BF16),
                            pltpu.VMEM((f, d), BF16),
                            pltpu.SemaphoreType.DMA((GATHER_SLOTS,)), pltpu.SemaphoreType.DMA(())],
        ),
        out_shape=jax.ShapeDtypeStruct((n_dst * SLAB, V7X_LANES), U32),
        compiler_params=pltpu.CompilerParams(dimension_semantics=("arbitrary",),
                                             vmem_limit_bytes=52 * 1024 * 1024,
                                             has_side_effects=True),
        name="moe_experts",
    )(src, dst, tile_expert, n_live, hn, w_gate, w_up, w_down)


def _combine_kernel(h_ref, w1_ref, w2_ref, fw_ref, y1_ref, y2_ref, o_ref, *, final_norm):
    tm = h_ref.shape[0]
    w1 = w1_ref[...]
    w2 = w2_ref[...]
    ssq = jnp.zeros((tm, 1), F32)
    y1 = _slab_load(y1_ref, tm)
    y2 = _slab_load(y2_ref, tm)
    for j in range(2 * SLAB):
        cols = slice(j * V7X_LANES, (j + 1) * V7X_LANES)
        out = h_ref[:, cols] + w1 * y1[j] + w2 * y2[j]
        o_ref[:, cols] = out
        ssq = ssq + jnp.sum(out * out, axis=-1, keepdims=True)
    if final_norm:
        inv = lax.rsqrt(ssq / D_MODEL + EPS)
        o_ref[...] = (o_ref[...] * inv) * fw_ref[...]


def _combine(h, ys, w1, w2, final_w, final_norm):
    t, d = h.shape
    tm = min(COMBINE_TM, t)
    nt = t // tm
    return pl.pallas_call(
        functools.partial(_combine_kernel, final_norm=final_norm),
        grid=(nt,),
        in_specs=[
            pl.BlockSpec((tm, d), lambda i: (i, 0)),
            pl.BlockSpec((tm, 1), lambda i: (i, 0)),
            pl.BlockSpec((tm, 1), lambda i: (i, 0)),
            pl.BlockSpec((1, d), lambda i: (0, 0)),
            pl.BlockSpec((tm * SLAB, V7X_LANES), lambda i: (i, 0)),
            pl.BlockSpec((tm * SLAB, V7X_LANES), lambda i: (i + nt, 0)),
        ],
        out_specs=pl.BlockSpec((tm, d), lambda i: (i, 0)),
        out_shape=jax.ShapeDtypeStruct((t, d), F32),
        compiler_params=_cparams(1, 32),
        name="moe_combine",
    )(h, w1.reshape(t, 1), w2.reshape(t, 1), final_w.reshape(1, d), ys, ys)


def _moe(h, nw, w_group, b_group, w_router, b_router, w_gate, w_up, w_down, layer, final_w,
         final_norm):
    t, d = h.shape
    hn, idx, gw, cnt = _route(h, nw, w_group, b_group, w_router, b_router)
    counts = cnt[:, 0]
    padded = ((counts + EXPERT_TM - 1) // EXPERT_TM) * EXPERT_TM
    seg_end = jnp.cumsum(padded)
    seg_start = seg_end - padded
    pos = jnp.concatenate([seg_start[idx[0]] + idx[2], seg_start[idx[1]] + idx[3]]).astype(I32)
    n_tiles = (2 * t) // EXPERT_TM + N_EXPERTS
    n_slots = n_tiles * EXPERT_TM
    n_live = (seg_end[-1] // EXPERT_TM).astype(I32)
    tile_first = jnp.minimum(jnp.arange(n_tiles, dtype=I32), n_live - 1) * EXPERT_TM
    tile_expert = jnp.sum(tile_first[:, None] >= seg_end[None, :], axis=1).astype(I32)
    n_dst = 2 * t + EXPERT_TM
    spare = 2 * t + jnp.arange(n_slots, dtype=I32) % EXPERT_TM
    dst = spare.at[pos].set(jnp.arange(2 * t, dtype=I32), unique_indices=True,
                            mode="promise_in_bounds")
    src = jnp.where(dst < 2 * t, dst % t, 0)
    ys = _experts(hn, src, dst, tile_expert, n_live.reshape(1), w_gate, w_up, w_down, layer, n_dst)
    return _combine(h, ys, gw[0], gw[1], final_w, final_norm)


def kernel(x, norm_mix_w, norm_ffn_w, final_norm_w, mix_w_in, pool_w, pool_scale, attn_sinks,
           mix_w_out, ssm_w_in, ssm_conv_w, ssm_conv_b, ssm_dt_bias, ssm_a_log, ssm_d,
           ssm_norm_w, ssm_w_out, moe_w_group, moe_b_group, moe_w_router, moe_b_router,
           moe_w_gate, moe_w_up, moe_w_down):
    batch, seq_len, d = x.shape
    h = x.reshape(batch * seq_len, d)

    proj = _norm_matmul(h, norm_mix_w[0], mix_w_in, MIX_IN_WIDTH, MIX_IN_WIDTH // 5, BF16)
    mixed = _pool_attn(proj, pool_w, pool_scale, attn_sinks, seq_len)
    h = _matmul_residual(mixed, mix_w_out, h, D_MODEL // 2)
    h = _moe(h, norm_ffn_w[0], moe_w_group[0], moe_b_group[0], moe_w_router[0], moe_b_router[0],
             moe_w_gate, moe_w_up, moe_w_down, 0, final_norm_w, False)

    ssm_w_in_t = jnp.swapaxes(ssm_w_in, 1, 2)
    zx = _norm_matmul(h, norm_mix_w[1], ssm_w_in_t, ZX_WIDTH, ZX_WIDTH // 10, BF16, True)
    dt_raw = _norm_matmul(h, norm_mix_w[1], ssm_w_in_t[:, ZX_WIDTH:, :], N_SSM_HEADS, N_SSM_HEADS,
                          F32, True)
    y = _ssd(zx, dt_raw, ssm_conv_w, ssm_conv_b, ssm_dt_bias, ssm_a_log, ssm_d, ssm_norm_w,
             batch, seq_len)
    h = _matmul_residual(y, ssm_w_out, h, D_MODEL // 4)
    h = _moe(h, norm_ffn_w[1], moe_w_group[1], moe_b_group[1], moe_w_router[1], moe_b_router[1],
             moe_w_gate, moe_w_up, moe_w_down, 1, final_norm_w, True)
    return h.reshape(batch, seq_len, d)
```

```python
import functools

import jax
import jax.numpy as jnp
from jax import lax
from jax.experimental import pallas as pl
from jax.experimental.pallas import tpu as pltpu

F32 = jnp.float32
BF16 = jnp.bfloat16
I32 = jnp.int32

D_MODEL = 2048
EPS = 1e-6
POOL_WINDOWS = (2, 4, 8, 16)
POOL_WIDTH = 1024
POOL_GROUP = 256
HEAD_DIM = 64
N_Q_HEADS = 16
N_KV_HEADS = 4
Q_PER_KV = 4
ATTN_WIDTH = 1024
KV_WIDTH = 256
WINDOW = 128
BLOCK = 128
MIX_IN_WIDTH = 2560
D_INNER = 4096
SSM_HEAD_DIM = 64
N_SSM_HEADS = 64
N_SSM_GROUPS = 8
HEADS_PER_GROUP = 8
D_STATE = 128
CONV_WIDTH = 4
CHUNK = 128
CONV_CH = D_INNER + 2 * N_SSM_GROUPS * D_STATE
ZX_WIDTH = D_INNER + CONV_CH
N_EXPERT_GROUPS = 4
EXPERTS_PER_GROUP = 4
N_EXPERTS = 16
D_FF_EXPERT = 512
SLOPES = tuple(float(2.0 ** (-8.0 * (i + 1) / N_Q_HEADS)) for i in range(N_Q_HEADS))

V7X_LANES = 128
V7X_SUBLANES = 8

PROJ_TM = 1024
NORM_ROWS = 128
ROUTE_TM = 512
EXPERT_TM = 256
COMBINE_TM = 256
ROUTE_ROWS = 32
EXPERT_ROW0 = 8
DMA_UNROLL = 32
GATHER_SLOTS = 3
N_BURSTS = 8
BURST_ROWS = EXPERT_TM // N_BURSTS
HALF = D_MODEL // 2
SLAB = HALF // V7X_LANES


def _cparams(n_axes, vmem_mb):
    return pltpu.CompilerParams(
        dimension_semantics=("arbitrary",) * n_axes,
        vmem_limit_bytes=vmem_mb * 1024 * 1024,
    )


def _silu(x):
    half = 0.5 * x
    return half + half * jnp.tanh(half)


def _split_bf16(x):
    hi = x.astype(BF16)
    lo = (x - hi.astype(F32)).astype(BF16)
    return hi, lo


def _slab_rows(j, n):
    return pl.ds(j, n, stride=SLAB)


U32 = jnp.uint32


def _pack_bf16_pair(lo, hi):
    lo_bits = lax.bitcast_convert_type(lo.astype(BF16).astype(F32), U32)
    hi_bits = lax.bitcast_convert_type(hi.astype(BF16).astype(F32), U32)
    return (lo_bits >> 16) | hi_bits


def _unpack_bf16_pair(word):
    lo = lax.bitcast_convert_type(word << 16, F32)
    hi = lax.bitcast_convert_type(word & jnp.uint32(0xFFFF0000), F32)
    return lo, hi


def _slab_store(ref, x):
    n = x.shape[0]
    for j in range(SLAB):
        lo = x[:, j * V7X_LANES:(j + 1) * V7X_LANES]
        hi = x[:, HALF + j * V7X_LANES:HALF + (j + 1) * V7X_LANES]
        ref[_slab_rows(j, n), :] = _pack_bf16_pair(lo, hi)


def _slab_load(ref, n):
    pairs = [_unpack_bf16_pair(ref[_slab_rows(j, n), :]) for j in range(SLAB)]
    return [p[0] for p in pairs] + [p[1] for p in pairs]


def _norm_matmul_kernel(x_ref, nw_ref, w_ref, o_ref, xn_ref, *, w_transposed):
    tm = x_ref.shape[0]

    @pl.when(pl.program_id(1) == 0)
    def _():
        def body(r, carry):
            rows = pl.ds(pl.multiple_of(r * NORM_ROWS, NORM_ROWS), NORM_ROWS)
            x = x_ref[rows, :]
            var = jnp.mean(x * x, axis=-1, keepdims=True)
            xn_ref[rows, :] = ((x * lax.rsqrt(var + EPS)) * nw_ref[...]).astype(BF16)
            return carry

        lax.fori_loop(0, tm // NORM_ROWS, body, 0)

    contract = ((1,), (1,)) if w_transposed else ((1,), (0,))
    o_ref[...] = lax.dot_general(xn_ref[...], w_ref[0].astype(BF16), (contract, ((), ())),
                                 preferred_element_type=F32).astype(o_ref.dtype)


def _norm_matmul(x, nw, w, n_out, tn, out_dtype, w_transposed=False):
    t, d = x.shape
    tm = min(PROJ_TM, t)
    if w_transposed:
        w_spec = pl.BlockSpec((1, tn, d), lambda i, j: (0, j, 0))
    else:
        w_spec = pl.BlockSpec((1, d, tn), lambda i, j: (0, 0, j))
    return pl.pallas_call(
        functools.partial(_norm_matmul_kernel, w_transposed=w_transposed),
        grid=(t // tm, n_out // tn),
        in_specs=[
            pl.BlockSpec((tm, d), lambda i, j: (i, 0)),
            pl.BlockSpec((1, d), lambda i, j: (0, 0)),
            w_spec,
        ],
        out_specs=pl.BlockSpec((tm, tn), lambda i, j: (i, j)),
        out_shape=jax.ShapeDtypeStruct((t, n_out), out_dtype),
        scratch_shapes=[pltpu.VMEM((tm, d), BF16)],
        compiler_params=_cparams(2, 56),
        name="norm_matmul",
    )(x, nw.reshape(1, d), w)


def _matmul_res_kernel(a_ref, w_ref, r_ref, o_ref):
    o_ref[...] = r_ref[...] + jnp.dot(a_ref[...], w_ref[0].astype(BF16),
                                      preferred_element_type=F32)


def _matmul_residual(a, w, res, tn):
    t, k = a.shape
    n = w.shape[2]
    tm = min(PROJ_TM, t)
    return pl.pallas_call(
        _matmul_res_kernel,
        grid=(t // tm, n // tn),
        in_specs=[
            pl.BlockSpec((tm, k), lambda i, j: (i, 0)),
            pl.BlockSpec((1, k, tn), lambda i, j: (0, 0, j)),
            pl.BlockSpec((tm, tn), lambda i, j: (i, j)),
        ],
        out_specs=pl.BlockSpec((tm, tn), lambda i, j: (i, j)),
        out_shape=jax.ShapeDtypeStruct((t, n), F32),
        compiler_params=_cparams(2, 56),
        name="matmul_residual",
    )(a, w, res)


def _pool_attn_kernel(sinks_ref, u_ref, up_ref, q_ref, k_ref, kp_ref, v_ref, vp_ref,
                      pw_ref, ps_ref, o_ref, *, blocks_per_seq):
    blk = pl.program_id(0) % blocks_per_seq
    first = blk == 0
    row = lax.broadcasted_iota(I32, (BLOCK, 2 * BLOCK), 0)
    col = lax.broadcasted_iota(I32, (BLOCK, 2 * BLOCK), 1)
    dist = row + BLOCK - col

    u_cur = u_ref[...]
    u_prev = jnp.where(first, jnp.zeros_like(u_cur), up_ref[...])
    u_ext = jnp.concatenate([u_prev, u_cur], axis=0)
    pos = blk * BLOCK + lax.broadcasted_iota(I32, (BLOCK, 1), 0)
    for g, w in enumerate(POOL_WINDOWS):
        cols = slice(g * POOL_GROUP, (g + 1) * POOL_GROUP)
        band = jnp.where((dist >= 0) & (dist < w), 1.0, 0.0).astype(BF16)
        wsum = jnp.dot(band, u_ext[:, cols], preferred_element_type=F32)
        count = jnp.minimum(pos + 1, w).astype(F32)
        pooled = (wsum / count - u_cur[:, cols].astype(F32)).astype(BF16)
        out_g = jnp.dot(pooled, pw_ref[0, g].astype(BF16), preferred_element_type=F32)
        o_ref[:, cols] = (out_g * ps_ref[:, cols]).astype(o_ref.dtype)

    kk = jnp.concatenate([kp_ref[...], k_ref[...]], axis=0)
    vv = jnp.concatenate([vp_ref[...], v_ref[...]], axis=0)
    valid = (dist >= 0) & (dist < WINDOW) & ((col >= BLOCK) | jnp.logical_not(first))
    dist_f = dist.astype(F32)
    scale = HEAD_DIM ** -0.5
    for pair in range(N_Q_HEADS // 2):
        q_pair = q_ref[:, pair * 2 * HEAD_DIM:(pair + 1) * 2 * HEAD_DIM]
        outs = []
        for sub in range(2):
            h = pair * 2 + sub
            g = h // Q_PER_KV
            qh = q_pair[:, sub * HEAD_DIM:(sub + 1) * HEAD_DIM]
            kg = kk[:, g * HEAD_DIM:(g + 1) * HEAD_DIM]
            vg = vv[:, g * HEAD_DIM:(g + 1) * HEAD_DIM]
            s = lax.dot_general(qh, kg, (((1,), (1,)), ((), ())),
                                preferred_element_type=F32) * scale
            s = jnp.where(valid, s - SLOPES[h] * dist_f, -jnp.inf)
            sink = sinks_ref[0, h]
            m = jnp.maximum(jnp.max(s, axis=-1, keepdims=True), sink)
            p = jnp.exp(s - m)
            denom = jnp.sum(p, axis=-1, keepdims=True) + jnp.exp(sink - m)
            probs = (p / denom).astype(BF16)
            outs.append(jnp.dot(probs, vg, preferred_element_type=F32))
        lo = ATTN_WIDTH + pair * 2 * HEAD_DIM
        o_ref[:, lo:lo + 2 * HEAD_DIM] = jnp.concatenate(outs, axis=1).astype(o_ref.dtype)


def _pool_attn(proj, pool_w, pool_scale, sinks, seq_len):
    t = proj.shape[0]
    nblk = t // BLOCK
    kcol = (POOL_WIDTH + ATTN_WIDTH) // KV_WIDTH

    def cur(cb):
        return lambda i: (i, cb)

    def prev(cb):
        return lambda i: (jnp.maximum(i - 1, 0), cb)

    return pl.pallas_call(
        functools.partial(_pool_attn_kernel, blocks_per_seq=seq_len // BLOCK),
        grid=(nblk,),
        in_specs=[
            pl.BlockSpec(memory_space=pltpu.SMEM),
            pl.BlockSpec((BLOCK, POOL_WIDTH), cur(0)),
            pl.BlockSpec((BLOCK, POOL_WIDTH), prev(0)),
            pl.BlockSpec((BLOCK, ATTN_WIDTH), cur(1)),
            pl.BlockSpec((BLOCK, KV_WIDTH), cur(kcol)),
            pl.BlockSpec((BLOCK, KV_WIDTH), prev(kcol)),
            pl.BlockSpec((BLOCK, KV_WIDTH), cur(kcol + 1)),
            pl.BlockSpec((BLOCK, KV_WIDTH), prev(kcol + 1)),
            pl.BlockSpec((1, len(POOL_WINDOWS), POOL_GROUP, POOL_GROUP), lambda i: (0, 0, 0, 0)),
            pl.BlockSpec((1, POOL_WIDTH), lambda i: (0, 0)),
        ],
        out_specs=pl.BlockSpec((BLOCK, POOL_WIDTH + ATTN_WIDTH), lambda i: (i, 0)),
        out_shape=jax.ShapeDtypeStruct((t, POOL_WIDTH + ATTN_WIDTH), BF16),
        compiler_params=_cparams(1, 32),
        name="pool_attn",
    )(sinks, proj, proj, proj, proj, proj, proj, proj, pool_w, pool_scale)


def _ssd_kernel(z_ref, x_ref, bc_ref, dt_ref, cw_ref, cb_ref, dtb_ref, alog_ref, dexp_ref,
                nw_ref, e2_ref, o_ref, prev_ref, state_ref, y_ref):
    gn = N_SSM_GROUPS * D_STATE
    taps = CONV_WIDTH - 1

    @pl.when(pl.program_id(1) == 0)
    def _():
        prev_ref[...] = jnp.zeros_like(prev_ref)
        state_ref[...] = jnp.zeros_like(state_ref)

    cur = jnp.concatenate([x_ref[...], bc_ref[...]], axis=1)
    cat = jnp.concatenate([prev_ref[...], cur], axis=0)
    prev_ref[...] = cur
    srow = lax.broadcasted_iota(I32, (taps * CHUNK, 2 * CHUNK), 0)
    scol = lax.broadcasted_iota(I32, (taps * CHUNK, 2 * CHUNK), 1)
    shift = jnp.where(scol + taps - srow // CHUNK == srow % CHUNK + CHUNK, 1.0, 0.0).astype(BF16)
    shifted = jnp.dot(shift, cat, preferred_element_type=F32)
    conv = cb_ref[...] + cw_ref[taps:taps + 1, :] * cur.astype(F32)
    for j in range(taps):
        conv = conv + cw_ref[j:j + 1, :] * shifted[j * CHUNK:(j + 1) * CHUNK, :]
    act = _silu(conv)
    xs = act[:, 0:D_INNER]
    xs_b = xs.astype(BF16)
    bm = act[:, D_INNER:D_INNER + gn]
    cm = act[:, D_INNER + gn:CONV_CH]

    dt_in = dt_ref[...] + dtb_ref[...]
    dt = jnp.maximum(dt_in, 0.0) + jnp.log(1.0 + jnp.exp(-jnp.abs(dt_in)))
    da = dt * (-jnp.exp(alog_ref[...]))
    row = lax.broadcasted_iota(I32, (CHUNK, CHUNK), 0)
    col = lax.broadcasted_iota(I32, (CHUNK, CHUNK), 1)
    causal = row >= col
    tri = jnp.where(causal, 1.0, 0.0).astype(BF16)
    da_hi, da_lo = _split_bf16(da)
    a_cum = (jnp.dot(tri, da_hi, preferred_element_type=F32)
             + jnp.dot(tri, da_lo, preferred_element_type=F32))
    e_cum = jnp.exp(a_cum)
    tr = jnp.concatenate([a_cum, dt], axis=1).T
    a_cum_t = tr[0:N_SSM_HEADS, :]
    dt_t = tr[N_SSM_HEADS:2 * N_SSM_HEADS, :]
    a_last_col = a_cum_t[:, CHUNK - 1:CHUNK]
    w_t = dt_t * jnp.exp(a_last_col - a_cum_t)
    dec8 = jnp.broadcast_to(e_cum[CHUNK - 1:CHUNK, :], (V7X_SUBLANES, N_SSM_HEADS))
    dec_hi, dec_lo = _split_bf16(dec8)
    dec_exp = (jnp.dot(dec_hi, e2_ref[...], preferred_element_type=F32)
               + jnp.dot(dec_lo, e2_ref[...], preferred_element_type=F32))[0:1, :]

    lane = lax.broadcasted_iota(I32, (CHUNK, 2 * SSM_HEAD_DIM), 1)
    left = lane < SSM_HEAD_DIM
    for g in range(N_SSM_GROUPS):
        b_g = bm[:, g * D_STATE:(g + 1) * D_STATE]
        c_g = cm[:, g * D_STATE:(g + 1) * D_STATE]
        bt_g = b_g.T
        cb_g = lax.dot_general(c_g.astype(BF16), b_g.astype(BF16),
                               (((1,), (1,)), ((), ())), preferred_element_type=F32)
        for k in range(HEADS_PER_GROUP // 2):
            slab = slice((g * 4 + k) * 2 * SSM_HEAD_DIM, (g * 4 + k + 1) * 2 * SSM_HEAD_DIM)
            lhs_y = []
            lhs_s = []
            for sub in range(2):
                h = g * HEADS_PER_GROUP + 2 * k + sub
                colb = jnp.broadcast_to(a_cum[:, h:h + 1], (CHUNK, CHUNK))
                rowb = jnp.broadcast_to(a_cum_t[h:h + 1, :], (CHUNK, CHUNK))
                decay = jnp.exp(jnp.where(causal, colb - rowb, -jnp.inf))
                m_h = cb_g * decay * jnp.broadcast_to(dt_t[h:h + 1, :], (CHUNK, CHUNK))
                lhs_y.append(m_h.astype(BF16))
                lhs_y.append((c_g * jnp.broadcast_to(e_cum[:, h:h + 1], (CHUNK, CHUNK))).astype(BF16))
                lhs_s.append((bt_g * jnp.broadcast_to(w_t[h:h + 1, :], (CHUNK, CHUNK))).astype(BF16))
            x_slab = xs_b[:, slab]
            s_slab = state_ref[:, slab]
            s_slab_b = s_slab.astype(BF16)
            zero = jnp.zeros_like(x_slab)
            x_l = jnp.where(left, x_slab, zero)
            x_r = jnp.where(left, zero, x_slab)
            s_l = jnp.where(left, s_slab_b, zero)
            s_r = jnp.where(left, zero, s_slab_b)
            y_pair = jnp.dot(jnp.concatenate(lhs_y, axis=1),
                             jnp.concatenate([x_l, s_l, x_r, s_r], axis=0),
                             preferred_element_type=F32)
            y_ref[:, slab] = y_pair
            d_state = jnp.dot(jnp.concatenate(lhs_s, axis=1),
                              jnp.concatenate([x_l, x_r], axis=0),
                              preferred_element_type=F32)
            state_ref[:, slab] = s_slab * dec_exp[:, slab] + d_state

    gsz = D_INNER // N_SSM_GROUPS
    for g in range(N_SSM_GROUPS):
        cols = slice(g * gsz, (g + 1) * gsz)
        y = y_ref[:, cols] + dexp_ref[:, cols] * xs[:, cols]
        gated = y * _silu(z_ref[:, cols].astype(F32))
        var = jnp.mean(gated * gated, axis=-1, keepdims=True)
        o_ref[:, cols] = ((gated * lax.rsqrt(var + EPS)) * nw_ref[:, cols]).astype(o_ref.dtype)


def _ssd(zx, dt_raw, conv_w, conv_b, dt_bias, a_log, d_skip, norm_w, batch, seq_len):
    t = zx.shape[0]
    nc = seq_len // CHUNK
    d_exp = jnp.repeat(d_skip, SSM_HEAD_DIM, axis=1)
    e2 = jnp.repeat(jnp.eye(N_SSM_HEADS, dtype=BF16), SSM_HEAD_DIM, axis=1)
    bc_col = (2 * D_INNER) // (2 * N_SSM_GROUPS * D_STATE)

    def rows(cb):
        return lambda b, c: (b * nc + c, cb)

    def whole(b, c):
        return (0, 0)

    return pl.pallas_call(
        _ssd_kernel,
        grid=(batch, nc),
        in_specs=[
            pl.BlockSpec((CHUNK, D_INNER), rows(0)),
            pl.BlockSpec((CHUNK, D_INNER), rows(1)),
            pl.BlockSpec((CHUNK, 2 * N_SSM_GROUPS * D_STATE), rows(bc_col)),
            pl.BlockSpec((CHUNK, N_SSM_HEADS), rows(0)),
            pl.BlockSpec((None, CONV_WIDTH, CONV_CH), lambda b, c: (0, 0, 0)),
            pl.BlockSpec((1, CONV_CH), whole),
            pl.BlockSpec((1, N_SSM_HEADS), whole),
            pl.BlockSpec((1, N_SSM_HEADS), whole),
            pl.BlockSpec((1, D_INNER), whole),
            pl.BlockSpec((1, D_INNER), whole),
            pl.BlockSpec((N_SSM_HEADS, D_INNER), whole),
        ],
        out_specs=pl.BlockSpec((CHUNK, D_INNER), rows(0)),
        out_shape=jax.ShapeDtypeStruct((t, D_INNER), BF16),
        scratch_shapes=[
            pltpu.VMEM((CHUNK, CONV_CH), BF16),
            pltpu.VMEM((D_STATE, D_INNER), F32),
            pltpu.VMEM((CHUNK, D_INNER), F32),
        ],
        compiler_params=_cparams(2, 48),
        name="ssd",
    )(zx, zx, zx, dt_raw, conv_w, conv_b, dt_bias, a_log, d_exp, norm_w, e2)


def _route_kernel(h_ref, nw_ref, wt_ref, bias_ref, hn_ref, idx_ref, gw_ref, cnt_ref, carry_ref):
    tm = h_ref.shape[0]

    @pl.when(pl.program_id(0) == 0)
    def _():
        carry_ref[...] = jnp.zeros_like(carry_ref)

    h = h_ref[...]
    var = jnp.mean(h * h, axis=-1, keepdims=True)
    hn = (h * lax.rsqrt(var + EPS)) * nw_ref[...]
    _slab_store(hn_ref, hn)

    nt = (((1,), (1,)), ((), ()))
    hn_hi, hn_lo = _split_bf16(hn)
    w_hi, w_lo = _split_bf16(wt_ref[...])
    logits = (lax.dot_general(w_hi, hn_hi, nt, preferred_element_type=F32)
              + lax.dot_general(w_hi, hn_lo, nt, preferred_element_type=F32)
              + lax.dot_general(w_lo, hn_hi, nt, preferred_element_type=F32)
              + bias_ref[...])

    def lrow(r):
        return logits[r:r + 1, :]

    gl = [lrow(g) for g in range(N_EXPERT_GROUPS)]
    gmax = functools.reduce(jnp.maximum, gl)
    gidx = jnp.full((1, tm), N_EXPERT_GROUPS - 1, I32)
    for g in range(N_EXPERT_GROUPS - 2, -1, -1):
        gidx = jnp.where(gl[g] == gmax, g, gidx)
    gsum = functools.reduce(jnp.add, [jnp.exp(x - gmax) for x in gl])
    g_weight = 1.0 / gsum

    el = []
    for e in range(EXPERTS_PER_GROUP):
        v = lrow(EXPERT_ROW0 + e)
        for g in range(1, N_EXPERT_GROUPS):
            v = jnp.where(gidx == g, lrow(EXPERT_ROW0 + g * EXPERTS_PER_GROUP + e), v)
        el.append(v)

    def first_argmax(vals):
        m = functools.reduce(jnp.maximum, vals)
        idx = jnp.full((1, tm), len(vals) - 1, I32)
        for e in range(len(vals) - 2, -1, -1):
            idx = jnp.where(vals[e] == m, e, idx)
        return m, idx

    m1, i1 = first_argmax(el)
    el2 = [jnp.where(i1 == e, -jnp.inf, el[e]) for e in range(EXPERTS_PER_GROUP)]
    m2, i2 = first_argmax(el2)
    e21 = jnp.exp(m2 - m1)
    w1 = g_weight / (1.0 + e21)
    w2 = g_weight * e21 / (1.0 + e21)
    eid1 = gidx * EXPERTS_PER_GROUP + i1
    eid2 = gidx * EXPERTS_PER_GROUP + i2

    erow = lax.broadcasted_iota(I32, (N_EXPERTS, tm), 0)
    oh1 = erow == eid1
    oh2 = erow == eid2
    member = jnp.where(oh1 | oh2, 1.0, 0.0)
    s_i = lax.broadcasted_iota(I32, (tm, tm), 0)
    t_i = lax.broadcasted_iota(I32, (tm, tm), 1)
    before = jnp.where(s_i < t_i, 1.0, 0.0).astype(BF16)
    excl = jnp.dot(member.astype(BF16), before, preferred_element_type=F32)
    carry = carry_ref[...]
    tot = excl + carry[:, 0:1]
    rank1 = jnp.sum(jnp.where(oh1, tot, 0.0), axis=0, keepdims=True)
    rank2 = jnp.sum(jnp.where(oh2, tot, 0.0), axis=0, keepdims=True)
    carry = carry + jnp.sum(member, axis=1, keepdims=True)
    carry_ref[...] = carry
    cnt_ref[...] = carry.astype(I32)

    idx_ref[...] = jnp.zeros_like(idx_ref)
    idx_ref[0:1, :] = eid1
    idx_ref[1:2, :] = eid2
    idx_ref[2:3, :] = rank1.astype(I32)
    idx_ref[3:4, :] = rank2.astype(I32)
    gw_ref[...] = jnp.zeros_like(gw_ref)
    gw_ref[0:1, :] = w1
    gw_ref[1:2, :] = w2


def _route(h, nw, w_group, b_group, w_router, b_router):
    t, d = h.shape
    tm = min(ROUTE_TM, t)
    wt = jnp.zeros((ROUTE_ROWS, d), F32)
    wt = wt.at[0:N_EXPERT_GROUPS].set(w_group.T)
    wt = wt.at[EXPERT_ROW0:EXPERT_ROW0 + N_EXPERTS].set(
        jnp.transpose(w_router, (0, 2, 1)).reshape(N_EXPERTS, d))
    bias = jnp.zeros((ROUTE_ROWS, 1), F32)
    bias = bias.at[0:N_EXPERT_GROUPS, 0].set(b_group)
    bias = bias.at[EXPERT_ROW0:EXPERT_ROW0 + N_EXPERTS, 0].set(b_router.reshape(N_EXPERTS))
    return pl.pallas_call(
        _route_kernel,
        grid=(t // tm,),
        in_specs=[
            pl.BlockSpec((tm, d), lambda i: (i, 0)),
            pl.BlockSpec((1, d), lambda i: (0, 0)),
            pl.BlockSpec((ROUTE_ROWS, d), lambda i: (0, 0)),
            pl.BlockSpec((ROUTE_ROWS, 1), lambda i: (0, 0)),
        ],
        out_specs=[
            pl.BlockSpec((tm * SLAB, V7X_LANES), lambda i: (i, 0)),
            pl.BlockSpec((V7X_SUBLANES, tm), lambda i: (0, i)),
            pl.BlockSpec((V7X_SUBLANES, tm), lambda i: (0, i)),
            pl.BlockSpec((N_EXPERTS, V7X_LANES), lambda i: (0, 0)),
        ],
        out_shape=[
            jax.ShapeDtypeStruct((t * SLAB, V7X_LANES), U32),
            jax.ShapeDtypeStruct((V7X_SUBLANES, t), I32),
            jax.ShapeDtypeStruct((V7X_SUBLANES, t), F32),
            jax.ShapeDtypeStruct((N_EXPERTS, V7X_LANES), I32),
        ],
        scratch_shapes=[pltpu.VMEM((N_EXPERTS, V7X_LANES), F32)],
        compiler_params=_cparams(1, 40),
        name="moe_route",
    )(h, nw.reshape(1, d), wt, bias)


def _expert_kernel(src_ref, dst_ref, te_ref, nv_ref, hn_hbm, wg_ref, wu_ref, wd_ref, out_hbm,
                   xbuf, ybuf, wg_b, wu_b, wd_b, gsems, ssems, *, n_dst):
    i = pl.program_id(0)
    n_live = nv_ref[0]
    live = i < n_live
    tm = EXPERT_TM

    def slab_at(r):
        return pl.ds(pl.multiple_of(r * SLAB, SLAB), SLAB)

    def gather_copy(tile_slot, r, tok):
        return pltpu.make_async_copy(hn_hbm.at[slab_at(tok)], xbuf.at[tile_slot, slab_at(r)],
                                     gsems.at[tile_slot])

    def scatter_copy(buf_slot, r, dst_row):
        return pltpu.make_async_copy(ybuf.at[buf_slot, slab_at(r)], out_hbm.at[slab_at(dst_row)],
                                     ssems.at[buf_slot])

    def start_gather_row(virtual_tile, r, priority):
        base = jnp.minimum(virtual_tile, n_live - 1) * tm
        gather_copy(virtual_tile % GATHER_SLOTS, r, src_ref[base + r]).start(priority=priority)

    def start_scatter_row(step, r, priority):
        scatter_copy((step + 1) % 2, r, dst_ref[step * tm + r]).start(priority=priority)

    def burst(b):
        for r in range(b * BURST_ROWS, (b + 1) * BURST_ROWS):
            start_scatter_row(i, r, r % 2)
            start_gather_row(i + GATHER_SLOTS - 1, r, r % 2)

    def gather_wait(tile_slot):
        def body(r, carry):
            gather_copy(tile_slot, 0, 0).wait()
            return carry

        lax.fori_loop(0, tm, body, 0, unroll=DMA_UNROLL)

    def scatter_wait(buf_slot):
        def body(r, carry):
            scatter_copy(buf_slot, 0, 0).wait()
            return carry

        lax.fori_loop(0, tm, body, 0, unroll=DMA_UNROLL)

    def spare_copy():
        return pltpu.make_async_copy(ybuf.at[0], out_hbm.at[pl.ds((n_dst - 2 * tm) * SLAB, tm * SLAB)],
                                     ssems.at[0])

    @pl.when(i == 0)
    def _():
        ybuf[...] = jnp.zeros_like(ybuf)
        spare_copy().start()
        for v in range(GATHER_SLOTS - 1):
            def body(rr, carry, v=v):
                for u in range(DMA_UNROLL):
                    start_gather_row(v, rr * DMA_UNROLL + u, u % 2)
                return carry

            lax.fori_loop(0, tm // DMA_UNROLL, body, 0)
        spare_copy().wait()

    changed = jnp.logical_or(i == 0, te_ref[i] != te_ref[jnp.maximum(i - 1, 0)])

    @pl.when(jnp.logical_and(live, changed))
    def _():
        wg_b[...] = wg_ref[0, 0].astype(BF16)
        wu_b[...] = wu_ref[0, 0].astype(BF16)
        wd_b[...] = wd_ref[0, 0].astype(BF16)

    @pl.when(live)
    def _():
        gather_wait(i % GATHER_SLOTS)
        x = jnp.concatenate([c.astype(BF16) for c in _slab_load(xbuf.at[i % GATHER_SLOTS], tm)],
                            axis=1)
        f = wg_b.shape[1]
        d = wd_b.shape[1]
        n_up = N_BURSTS // 4
        n_down = N_BURSTS // 2
        b = 0
        gate, up = [], []
        for w_b, parts in ((wg_b, gate), (wu_b, up)):
            for c in range(n_up):
                burst(b)
                b += 1
                cols = slice(c * (f // n_up), (c + 1) * (f // n_up))
                parts.append(jnp.dot(x, w_b[:, cols], preferred_element_type=F32))
        act = (_silu(jnp.concatenate(gate, axis=1)) * jnp.concatenate(up, axis=1)).astype(BF16)
        y = []
        for c in range(n_down):
            burst(b)
            b += 1
            cols = slice(c * (d // n_down), (c + 1) * (d // n_down))
            y.append(jnp.dot(act, wd_b[:, cols], preferred_element_type=F32))

        @pl.when(i > 0)
        def _():
            scatter_wait(i % 2)

        _slab_store(ybuf.at[i % 2], jnp.concatenate(y, axis=1))

    @pl.when(i == n_live)
    def _():
        scatter_wait(i % 2)

        def body(rr, carry):
            for u in range(DMA_UNROLL):
                start_scatter_row(i, rr * DMA_UNROLL + u, u % 2)
            return carry

        lax.fori_loop(0, tm // DMA_UNROLL, body, 0)
        scatter_wait((i + 1) % 2)
        gather_wait(i % GATHER_SLOTS)
        gather_wait((i + 1) % GATHER_SLOTS)


def _experts(hn, src, dst, tile_expert, n_live, w_gate, w_up, w_down, layer, n_dst):
    d = D_MODEL
    n_slots = src.shape[0]
    f = w_gate.shape[3]

    def wsel(i, src_r, dst_r, te, nv):
        return (layer, te[i], 0, 0)

    return pl.pallas_call(
        functools.partial(_expert_kernel, n_dst=n_dst),
        grid_spec=pltpu.PrefetchScalarGridSpec(
            num_scalar_prefetch=4,
            grid=(n_slots // EXPERT_TM + 1,),
            in_specs=[
                pl.BlockSpec(memory_space=pl.ANY),
                pl.BlockSpec((1, 1, d, f), wsel),
                pl.BlockSpec((1, 1, d, f), wsel),
                pl.BlockSpec((1, 1, f, d), wsel),
            ],
            out_specs=pl.BlockSpec(memory_space=pl.ANY),
            scratch_shapes=[pltpu.VMEM((GATHER_SLOTS, EXPERT_TM * SLAB, V7X_LANES), U32),
                            pltpu.VMEM((2, EXPERT_TM * SLAB, V7X_LANES), U32),
                            pltpu.VMEM((d, f), BF16), pltpu.VMEM((d, f), BF16),
                            pltpu.VMEM((f, d), BF16),
                            pltpu.SemaphoreType.DMA((GATHER_SLOTS,)), pltpu.SemaphoreType.DMA((2,))],
        ),
        out_shape=jax.ShapeDtypeStruct((n_dst * SLAB, V7X_LANES), U32),
        compiler_params=pltpu.CompilerParams(dimension_semantics=("arbitrary",),
                                             vmem_limit_bytes=52 * 1024 * 1024,
                                             has_side_effects=True),
        name="moe_experts",
    )(src, dst, tile_expert, n_live, hn, w_gate, w_up, w_down)


def _combine_kernel(h_ref, w1_ref, w2_ref, fw_ref, y1_ref, y2_ref, o_ref, *, final_norm):
    tm = h_ref.shape[0]
    w1 = w1_ref[...]
    w2 = w2_ref[...]
    ssq = jnp.zeros((tm, 1), F32)
    y1 = _slab_load(y1_ref, tm)
    y2 = _slab_load(y2_ref, tm)
    for j in range(2 * SLAB):
        cols = slice(j * V7X_LANES, (j + 1) * V7X_LANES)
        out = h_ref[:, cols] + w1 * y1[j] + w2 * y2[j]
        o_ref[:, cols] = out
        ssq = ssq + jnp.sum(out * out, axis=-1, keepdims=True)
    if final_norm:
        inv = lax.rsqrt(ssq / D_MODEL + EPS)
        o_ref[...] = (o_ref[...] * inv) * fw_ref[...]


def _combine(h, ys, w1, w2, final_w, final_norm):
    t, d = h.shape
    tm = min(COMBINE_TM, t)
    nt = t // tm
    return pl.pallas_call(
        functools.partial(_combine_kernel, final_norm=final_norm),
        grid=(nt,),
        in_specs=[
            pl.BlockSpec((tm, d), lambda i: (i, 0)),
            pl.BlockSpec((tm, 1), lambda i: (i, 0)),
            pl.BlockSpec((tm, 1), lambda i: (i, 0)),
            pl.BlockSpec((1, d), lambda i: (0, 0)),
            pl.BlockSpec((tm * SLAB, V7X_LANES), lambda i: (i, 0)),
            pl.BlockSpec((tm * SLAB, V7X_LANES), lambda i: (i + nt, 0)),
        ],
        out_specs=pl.BlockSpec((tm, d), lambda i: (i, 0)),
        out_shape=jax.ShapeDtypeStruct((t, d), F32),
        compiler_params=_cparams(1, 32),
        name="moe_combine",
    )(h, w1.reshape(t, 1), w2.reshape(t, 1), final_w.reshape(1, d), ys, ys)


def _moe(h, nw, w_group, b_group, w_router, b_router, w_gate, w_up, w_down, layer, final_w,
         final_norm):
    t, d = h.shape
    hn, idx, gw, cnt = _route(h, nw, w_group, b_group, w_router, b_router)
    counts = cnt[:, 0]
    padded = ((counts + EXPERT_TM - 1) // EXPERT_TM) * EXPERT_TM
    seg_end = jnp.cumsum(padded)
    seg_start = seg_end - padded
    pos = jnp.concatenate([seg_start[idx[0]] + idx[2], seg_start[idx[1]] + idx[3]]).astype(I32)
    n_tiles = (2 * t) // EXPERT_TM + N_EXPERTS
    n_slots = n_tiles * EXPERT_TM
    n_live = (seg_end[-1] // EXPERT_TM).astype(I32)
    tile_first = jnp.minimum(jnp.arange(n_tiles + 1, dtype=I32), n_live - 1) * EXPERT_TM
    tile_expert = jnp.sum(tile_first[:, None] >= seg_end[None, :], axis=1).astype(I32)
    n_dst = 2 * t + 2 * EXPERT_TM
    spare = 2 * t + jnp.arange(n_slots, dtype=I32) % EXPERT_TM
    dst = spare.at[pos].set(jnp.arange(2 * t, dtype=I32), unique_indices=True,
                            mode="promise_in_bounds")
    src = jnp.where(dst < 2 * t, dst % t, 0)
    dst = jnp.concatenate([2 * t + EXPERT_TM + jnp.arange(EXPERT_TM, dtype=I32), dst])
    ys = _experts(hn, src, dst, tile_expert, n_live.reshape(1), w_gate, w_up, w_down, layer, n_dst)
    return _combine(h, ys, gw[0], gw[1], final_w, final_norm)


def kernel(x, norm_mix_w, norm_ffn_w, final_norm_w, mix_w_in, pool_w, pool_scale, attn_sinks,
           mix_w_out, ssm_w_in, ssm_conv_w, ssm_conv_b, ssm_dt_bias, ssm_a_log, ssm_d,
           ssm_norm_w, ssm_w_out, moe_w_group, moe_b_group, moe_w_router, moe_b_router,
           moe_w_gate, moe_w_up, moe_w_down):
    batch, seq_len, d = x.shape
    h = x.reshape(batch * seq_len, d)

    proj = _norm_matmul(h, norm_mix_w[0], mix_w_in, MIX_IN_WIDTH, MIX_IN_WIDTH // 5, BF16)
    mixed = _pool_attn(proj, pool_w, pool_scale, attn_sinks, seq_len)
    h = _matmul_residual(mixed, mix_w_out, h, D_MODEL // 2)
    h = _moe(h, norm_ffn_w[0], moe_w_group[0], moe_b_group[0], moe_w_router[0], moe_b_router[0],
             moe_w_gate, moe_w_up, moe_w_down, 0, final_norm_w, False)

    ssm_w_in_t = jnp.swapaxes(ssm_w_in, 1, 2)
    zx = _norm_matmul(h, norm_mix_w[1], ssm_w_in_t, ZX_WIDTH, ZX_WIDTH // 10, BF16, True)
    dt_raw = _norm_matmul(h, norm_mix_w[1], ssm_w_in_t[:, ZX_WIDTH:, :], N_SSM_HEADS, N_SSM_HEADS,
                          F32, True)
    y = _ssd(zx, dt_raw, ssm_conv_w, ssm_conv_b, ssm_dt_bias, ssm_a_log, ssm_d, ssm_norm_w,
             batch, seq_len)
    h = _matmul_residual(y, ssm_w_out, h, D_MODEL // 4)
    h = _moe(h, norm_ffn_w[1], moe_w_group[1], moe_b_group[1], moe_w_router[1], moe_b_router[1],
             moe_w_gate, moe_w_up, moe_w_down, 1, final_norm_w, True)
    return h.reshape(batch, seq_len, d)
```

```python
import functools

import jax
import jax.numpy as jnp
from jax import lax
from jax.experimental import pallas as pl
from jax.experimental.pallas import tpu as pltpu

F32 = jnp.float32
BF16 = jnp.bfloat16
I32 = jnp.int32

D_MODEL = 2048
EPS = 1e-6
POOL_WINDOWS = (2, 4, 8, 16)
POOL_WIDTH = 1024
POOL_GROUP = 256
HEAD_DIM = 64
N_Q_HEADS = 16
N_KV_HEADS = 4
Q_PER_KV = 4
ATTN_WIDTH = 1024
KV_WIDTH = 256
WINDOW = 128
BLOCK = 128
MIX_IN_WIDTH = 2560
D_INNER = 4096
SSM_HEAD_DIM = 64
N_SSM_HEADS = 64
N_SSM_GROUPS = 8
HEADS_PER_GROUP = 8
D_STATE = 128
CONV_WIDTH = 4
CHUNK = 128
CONV_CH = D_INNER + 2 * N_SSM_GROUPS * D_STATE
ZX_WIDTH = D_INNER + CONV_CH
N_EXPERT_GROUPS = 4
EXPERTS_PER_GROUP = 4
N_EXPERTS = 16
D_FF_EXPERT = 512
SLOPES = tuple(float(2.0 ** (-8.0 * (i + 1) / N_Q_HEADS)) for i in range(N_Q_HEADS))

V7X_LANES = 128
V7X_SUBLANES = 8

PROJ_TM = 1024
NORM_ROWS = 128
ROUTE_TM = 512
EXPERT_TM = 256
COMBINE_TM = 256
ROUTE_ROWS = 32
EXPERT_ROW0 = 8
DMA_UNROLL = 32
GATHER_SLOTS = 3
N_BURSTS = 8
BURST_ROWS = EXPERT_TM // N_BURSTS
HALF = D_MODEL // 2
SLAB = HALF // V7X_LANES


def _cparams(n_axes, vmem_mb):
    return pltpu.CompilerParams(
        dimension_semantics=("arbitrary",) * n_axes,
        vmem_limit_bytes=vmem_mb * 1024 * 1024,
    )


def _silu(x):
    half = 0.5 * x
    return half + half * jnp.tanh(half)


def _split_bf16(x):
    hi = x.astype(BF16)
    lo = (x - hi.astype(F32)).astype(BF16)
    return hi, lo


def _slab_rows(j, n):
    return pl.ds(j, n, stride=SLAB)


U32 = jnp.uint32


def _pack_bf16_pair(lo, hi):
    lo_bits = lax.bitcast_convert_type(lo.astype(BF16).astype(F32), U32)
    hi_bits = lax.bitcast_convert_type(hi.astype(BF16).astype(F32), U32)
    return (lo_bits >> 16) | hi_bits


def _unpack_bf16_pair(word):
    lo = lax.bitcast_convert_type(word << 16, F32)
    hi = lax.bitcast_convert_type(word & jnp.uint32(0xFFFF0000), F32)
    return lo, hi


def _slab_store(ref, x):
    n = x.shape[0]
    for j in range(SLAB):
        lo = x[:, j * V7X_LANES:(j + 1) * V7X_LANES]
        hi = x[:, HALF + j * V7X_LANES:HALF + (j + 1) * V7X_LANES]
        ref[_slab_rows(j, n), :] = _pack_bf16_pair(lo, hi)


def _slab_load(ref, n):
    pairs = [_unpack_bf16_pair(ref[_slab_rows(j, n), :]) for j in range(SLAB)]
    return [p[0] for p in pairs] + [p[1] for p in pairs]


def _norm_matmul_kernel(x_ref, nw_ref, w_ref, o_ref, xn_ref, *, w_transposed):
    tm = x_ref.shape[0]

    @pl.when(pl.program_id(1) == 0)
    def _():
        def body(r, carry):
            rows = pl.ds(pl.multiple_of(r * NORM_ROWS, NORM_ROWS), NORM_ROWS)
            x = x_ref[rows, :]
            var = jnp.mean(x * x, axis=-1, keepdims=True)
            xn_ref[rows, :] = ((x * lax.rsqrt(var + EPS)) * nw_ref[...]).astype(BF16)
            return carry

        lax.fori_loop(0, tm // NORM_ROWS, body, 0)

    contract = ((1,), (1,)) if w_transposed else ((1,), (0,))
    o_ref[...] = lax.dot_general(xn_ref[...], w_ref[0].astype(BF16), (contract, ((), ())),
                                 preferred_element_type=F32).astype(o_ref.dtype)


def _norm_matmul(x, nw, w, n_out, tn, out_dtype, w_transposed=False):
    t, d = x.shape
    tm = min(PROJ_TM, t)
    if w_transposed:
        w_spec = pl.BlockSpec((1, tn, d), lambda i, j: (0, j, 0))
    else:
        w_spec = pl.BlockSpec((1, d, tn), lambda i, j: (0, 0, j))
    return pl.pallas_call(
        functools.partial(_norm_matmul_kernel, w_transposed=w_transposed),
        grid=(t // tm, n_out // tn),
        in_specs=[
            pl.BlockSpec((tm, d), lambda i, j: (i, 0)),
            pl.BlockSpec((1, d), lambda i, j: (0, 0)),
            w_spec,
        ],
        out_specs=pl.BlockSpec((tm, tn), lambda i, j: (i, j)),
        out_shape=jax.ShapeDtypeStruct((t, n_out), out_dtype),
        scratch_shapes=[pltpu.VMEM((tm, d), BF16)],
        compiler_params=_cparams(2, 56),
        name="norm_matmul",
    )(x, nw.reshape(1, d), w)


def _matmul_res_kernel(a_ref, w_ref, r_ref, o_ref):
    o_ref[...] = r_ref[...] + jnp.dot(a_ref[...], w_ref[0].astype(BF16),
                                      preferred_element_type=F32)


def _matmul_residual(a, w, res, tn):
    t, k = a.shape
    n = w.shape[2]
    tm = min(PROJ_TM, t)
    return pl.pallas_call(
        _matmul_res_kernel,
        grid=(t // tm, n // tn),
        in_specs=[
            pl.BlockSpec((tm, k), lambda i, j: (i, 0)),
            pl.BlockSpec((1, k, tn), lambda i, j: (0, 0, j)),
            pl.BlockSpec((tm, tn), lambda i, j: (i, j)),
        ],
        out_specs=pl.BlockSpec((tm, tn), lambda i, j: (i, j)),
        out_shape=jax.ShapeDtypeStruct((t, n), F32),
        compiler_params=_cparams(2, 56),
        name="matmul_residual",
    )(a, w, res)


def _pool_attn_kernel(sinks_ref, u_ref, up_ref, q_ref, k_ref, kp_ref, v_ref, vp_ref,
                      pw_ref, ps_ref, o_ref, *, blocks_per_seq):
    blk = pl.program_id(0) % blocks_per_seq
    first = blk == 0
    row = lax.broadcasted_iota(I32, (BLOCK, 2 * BLOCK), 0)
    col = lax.broadcasted_iota(I32, (BLOCK, 2 * BLOCK), 1)
    dist = row + BLOCK - col

    u_cur = u_ref[...]
    u_prev = jnp.where(first, jnp.zeros_like(u_cur), up_ref[...])
    u_ext = jnp.concatenate([u_prev, u_cur], axis=0)
    pos = blk * BLOCK + lax.broadcasted_iota(I32, (BLOCK, 1), 0)
    for g, w in enumerate(POOL_WINDOWS):
        cols = slice(g * POOL_GROUP, (g + 1) * POOL_GROUP)
        band = jnp.where((dist >= 0) & (dist < w), 1.0, 0.0).astype(BF16)
        wsum = jnp.dot(band, u_ext[:, cols], preferred_element_type=F32)
        count = jnp.minimum(pos + 1, w).astype(F32)
        pooled = (wsum / count - u_cur[:, cols].astype(F32)).astype(BF16)
        out_g = jnp.dot(pooled, pw_ref[0, g].astype(BF16), preferred_element_type=F32)
        o_ref[:, cols] = (out_g * ps_ref[:, cols]).astype(o_ref.dtype)

    kk = jnp.concatenate([kp_ref[...], k_ref[...]], axis=0)
    vv = jnp.concatenate([vp_ref[...], v_ref[...]], axis=0)
    valid = (dist >= 0) & (dist < WINDOW) & ((col >= BLOCK) | jnp.logical_not(first))
    dist_f = dist.astype(F32)
    scale = HEAD_DIM ** -0.5
    for pair in range(N_Q_HEADS // 2):
        q_pair = q_ref[:, pair * 2 * HEAD_DIM:(pair + 1) * 2 * HEAD_DIM]
        outs = []
        for sub in range(2):
            h = pair * 2 + sub
            g = h // Q_PER_KV
            qh = q_pair[:, sub * HEAD_DIM:(sub + 1) * HEAD_DIM]
            kg = kk[:, g * HEAD_DIM:(g + 1) * HEAD_DIM]
            vg = vv[:, g * HEAD_DIM:(g + 1) * HEAD_DIM]
            s = lax.dot_general(qh, kg, (((1,), (1,)), ((), ())),
                                preferred_element_type=F32) * scale
            s = jnp.where(valid, s - SLOPES[h] * dist_f, -jnp.inf)
            sink = sinks_ref[0, h]
            m = jnp.maximum(jnp.max(s, axis=-1, keepdims=True), sink)
            p = jnp.exp(s - m)
            denom = jnp.sum(p, axis=-1, keepdims=True) + jnp.exp(sink - m)
            probs = (p / denom).astype(BF16)
            outs.append(jnp.dot(probs, vg, preferred_element_type=F32))
        lo = ATTN_WIDTH + pair * 2 * HEAD_DIM
        o_ref[:, lo:lo + 2 * HEAD_DIM] = jnp.concatenate(outs, axis=1).astype(o_ref.dtype)


def _pool_attn(proj, pool_w, pool_scale, sinks, seq_len):
    t = proj.shape[0]
    nblk = t // BLOCK
    kcol = (POOL_WIDTH + ATTN_WIDTH) // KV_WIDTH

    def cur(cb):
        return lambda i: (i, cb)

    def prev(cb):
        return lambda i: (jnp.maximum(i - 1, 0), cb)

    return pl.pallas_call(
        functools.partial(_pool_attn_kernel, blocks_per_seq=seq_len // BLOCK),
        grid=(nblk,),
        in_specs=[
            pl.BlockSpec(memory_space=pltpu.SMEM),
            pl.BlockSpec((BLOCK, POOL_WIDTH), cur(0)),
            pl.BlockSpec((BLOCK, POOL_WIDTH), prev(0)),
            pl.BlockSpec((BLOCK, ATTN_WIDTH), cur(1)),
            pl.BlockSpec((BLOCK, KV_WIDTH), cur(kcol)),
            pl.BlockSpec((BLOCK, KV_WIDTH), prev(kcol)),
            pl.BlockSpec((BLOCK, KV_WIDTH), cur(kcol + 1)),
            pl.BlockSpec((BLOCK, KV_WIDTH), prev(kcol + 1)),
            pl.BlockSpec((1, len(POOL_WINDOWS), POOL_GROUP, POOL_GROUP), lambda i: (0, 0, 0, 0)),
            pl.BlockSpec((1, POOL_WIDTH), lambda i: (0, 0)),
        ],
        out_specs=pl.BlockSpec((BLOCK, POOL_WIDTH + ATTN_WIDTH), lambda i: (i, 0)),
        out_shape=jax.ShapeDtypeStruct((t, POOL_WIDTH + ATTN_WIDTH), BF16),
        compiler_params=_cparams(1, 32),
        name="pool_attn",
    )(sinks, proj, proj, proj, proj, proj, proj, proj, pool_w, pool_scale)


def _ssd_kernel(z_ref, x_ref, bc_ref, dt_ref, cw_ref, cb_ref, dtb_ref, alog_ref, dexp_ref,
                nw_ref, e2_ref, o_ref, prev_ref, state_ref, y_ref):
    gn = N_SSM_GROUPS * D_STATE
    taps = CONV_WIDTH - 1

    @pl.when(pl.program_id(1) == 0)
    def _():
        prev_ref[...] = jnp.zeros_like(prev_ref)
        state_ref[...] = jnp.zeros_like(state_ref)

    cur = jnp.concatenate([x_ref[...], bc_ref[...]], axis=1)
    cat = jnp.concatenate([prev_ref[...], cur], axis=0)
    prev_ref[...] = cur
    srow = lax.broadcasted_iota(I32, (taps * CHUNK, 2 * CHUNK), 0)
    scol = lax.broadcasted_iota(I32, (taps * CHUNK, 2 * CHUNK), 1)
    shift = jnp.where(scol + taps - srow // CHUNK == srow % CHUNK + CHUNK, 1.0, 0.0).astype(BF16)
    shifted = jnp.dot(shift, cat, preferred_element_type=F32)
    conv = cb_ref[...] + cw_ref[taps:taps + 1, :] * cur.astype(F32)
    for j in range(taps):
        conv = conv + cw_ref[j:j + 1, :] * shifted[j * CHUNK:(j + 1) * CHUNK, :]
    act = _silu(conv)
    xs = act[:, 0:D_INNER]
    xs_b = xs.astype(BF16)
    bm = act[:, D_INNER:D_INNER + gn]
    cm = act[:, D_INNER + gn:CONV_CH]

    dt_in = dt_ref[...] + dtb_ref[...]
    dt = jnp.maximum(dt_in, 0.0) + jnp.log(1.0 + jnp.exp(-jnp.abs(dt_in)))
    da = dt * (-jnp.exp(alog_ref[...]))
    row = lax.broadcasted_iota(I32, (CHUNK, CHUNK), 0)
    col = lax.broadcasted_iota(I32, (CHUNK, CHUNK), 1)
    causal = row >= col
    tri = jnp.where(causal, 1.0, 0.0).astype(BF16)
    da_hi, da_lo = _split_bf16(da)
    a_cum = (jnp.dot(tri, da_hi, preferred_element_type=F32)
             + jnp.dot(tri, da_lo, preferred_element_type=F32))
    e_cum = jnp.exp(a_cum)
    tr = jnp.concatenate([a_cum, dt], axis=1).T
    a_cum_t = tr[0:N_SSM_HEADS, :]
    dt_t = tr[N_SSM_HEADS:2 * N_SSM_HEADS, :]
    a_last_col = a_cum_t[:, CHUNK - 1:CHUNK]
    w_t = dt_t * jnp.exp(a_last_col - a_cum_t)
    dec8 = jnp.broadcast_to(e_cum[CHUNK - 1:CHUNK, :], (V7X_SUBLANES, N_SSM_HEADS))
    dec_hi, dec_lo = _split_bf16(dec8)
    dec_exp = (jnp.dot(dec_hi, e2_ref[...], preferred_element_type=F32)
               + jnp.dot(dec_lo, e2_ref[...], preferred_element_type=F32))[0:1, :]

    lane = lax.broadcasted_iota(I32, (CHUNK, 2 * SSM_HEAD_DIM), 1)
    left = lane < SSM_HEAD_DIM
    for g in range(N_SSM_GROUPS):
        b_g = bm[:, g * D_STATE:(g + 1) * D_STATE]
        c_g = cm[:, g * D_STATE:(g + 1) * D_STATE]
        bt_g = b_g.T
        cb_g = lax.dot_general(c_g.astype(BF16), b_g.astype(BF16),
                               (((1,), (1,)), ((), ())), preferred_element_type=F32)
        for k in range(HEADS_PER_GROUP // 2):
            slab = slice((g * 4 + k) * 2 * SSM_HEAD_DIM, (g * 4 + k + 1) * 2 * SSM_HEAD_DIM)
            lhs_y = []
            lhs_s = []
            for sub in range(2):
                h = g * HEADS_PER_GROUP + 2 * k + sub
                colb = jnp.broadcast_to(a_cum[:, h:h + 1], (CHUNK, CHUNK))
                rowb = jnp.broadcast_to(a_cum_t[h:h + 1, :], (CHUNK, CHUNK))
                decay = jnp.exp(jnp.where(causal, colb - rowb, -jnp.inf))
                m_h = cb_g * decay * jnp.broadcast_to(dt_t[h:h + 1, :], (CHUNK, CHUNK))
                lhs_y.append(m_h.astype(BF16))
                lhs_y.append((c_g * jnp.broadcast_to(e_cum[:, h:h + 1], (CHUNK, CHUNK))).astype(BF16))
                lhs_s.append((bt_g * jnp.broadcast_to(w_t[h:h + 1, :], (CHUNK, CHUNK))).astype(BF16))
            x_slab = xs_b[:, slab]
            s_slab = state_ref[:, slab]
            s_slab_b = s_slab.astype(BF16)
            zero = jnp.zeros_like(x_slab)
            x_l = jnp.where(left, x_slab, zero)
            x_r = jnp.where(left, zero, x_slab)
            s_l = jnp.where(left, s_slab_b, zero)
            s_r = jnp.where(left, zero, s_slab_b)
            y_pair = jnp.dot(jnp.concatenate(lhs_y, axis=1),
                             jnp.concatenate([x_l, s_l, x_r, s_r], axis=0),
                             preferred_element_type=F32)
            y_ref[:, slab] = y_pair
            d_state = jnp.dot(jnp.concatenate(lhs_s, axis=1),
                              jnp.concatenate([x_l, x_r], axis=0),
                              preferred_element_type=F32)
            state_ref[:, slab] = s_slab * dec_exp[:, slab] + d_state

    gsz = D_INNER // N_SSM_GROUPS
    for g in range(N_SSM_GROUPS):
        cols = slice(g * gsz, (g + 1) * gsz)
        y = y_ref[:, cols] + dexp_ref[:, cols] * xs[:, cols]
        gated = y * _silu(z_ref[:, cols].astype(F32))
        var = jnp.mean(gated * gated, axis=-1, keepdims=True)
        o_ref[:, cols] = ((gated * lax.rsqrt(var + EPS)) * nw_ref[:, cols]).astype(o_ref.dtype)


def _ssd(zx, dt_raw, conv_w, conv_b, dt_bias, a_log, d_skip, norm_w, batch, seq_len):
    t = zx.shape[0]
    nc = seq_len // CHUNK
    d_exp = jnp.repeat(d_skip, SSM_HEAD_DIM, axis=1)
    e2 = jnp.repeat(jnp.eye(N_SSM_HEADS, dtype=BF16), SSM_HEAD_DIM, axis=1)
    bc_col = (2 * D_INNER) // (2 * N_SSM_GROUPS * D_STATE)

    def rows(cb):
        return lambda b, c: (b * nc + c, cb)

    def whole(b, c):
        return (0, 0)

    return pl.pallas_call(
        _ssd_kernel,
        grid=(batch, nc),
        in_specs=[
            pl.BlockSpec((CHUNK, D_INNER), rows(0)),
            pl.BlockSpec((CHUNK, D_INNER), rows(1)),
            pl.BlockSpec((CHUNK, 2 * N_SSM_GROUPS * D_STATE), rows(bc_col)),
            pl.BlockSpec((CHUNK, N_SSM_HEADS), rows(0)),
            pl.BlockSpec((None, CONV_WIDTH, CONV_CH), lambda b, c: (0, 0, 0)),
            pl.BlockSpec((1, CONV_CH), whole),
            pl.BlockSpec((1, N_SSM_HEADS), whole),
            pl.BlockSpec((1, N_SSM_HEADS), whole),
            pl.BlockSpec((1, D_INNER), whole),
            pl.BlockSpec((1, D_INNER), whole),
            pl.BlockSpec((N_SSM_HEADS, D_INNER), whole),
        ],
        out_specs=pl.BlockSpec((CHUNK, D_INNER), rows(0)),
        out_shape=jax.ShapeDtypeStruct((t, D_INNER), BF16),
        scratch_shapes=[
            pltpu.VMEM((CHUNK, CONV_CH), BF16),
            pltpu.VMEM((D_STATE, D_INNER), F32),
            pltpu.VMEM((CHUNK, D_INNER), F32),
        ],
        compiler_params=_cparams(2, 48),
        name="ssd",
    )(zx, zx, zx, dt_raw, conv_w, conv_b, dt_bias, a_log, d_exp, norm_w, e2)


def _route_kernel(h_ref, nw_ref, wt_ref, bias_ref, hn_ref, idx_ref, gw_ref, cnt_ref, carry_ref):
    tm = h_ref.shape[0]

    @pl.when(pl.program_id(0) == 0)
    def _():
        carry_ref[...] = jnp.zeros_like(carry_ref)

    h = h_ref[...]
    var = jnp.mean(h * h, axis=-1, keepdims=True)
    hn = (h * lax.rsqrt(var + EPS)) * nw_ref[...]
    _slab_store(hn_ref, hn)

    nt = (((1,), (1,)), ((), ()))
    hn_hi, hn_lo = _split_bf16(hn)
    w_hi, w_lo = _split_bf16(wt_ref[...])
    logits = (lax.dot_general(w_hi, hn_hi, nt, preferred_element_type=F32)
              + lax.dot_general(w_hi, hn_lo, nt, preferred_element_type=F32)
              + lax.dot_general(w_lo, hn_hi, nt, preferred_element_type=F32)
              + bias_ref[...])

    def lrow(r):
        return logits[r:r + 1, :]

    gl = [lrow(g) for g in range(N_EXPERT_GROUPS)]
    gmax = functools.reduce(jnp.maximum, gl)
    gidx = jnp.full((1, tm), N_EXPERT_GROUPS - 1, I32)
    for g in range(N_EXPERT_GROUPS - 2, -1, -1):
        gidx = jnp.where(gl[g] == gmax, g, gidx)
    gsum = functools.reduce(jnp.add, [jnp.exp(x - gmax) for x in gl])
    g_weight = 1.0 / gsum

    el = []
    for e in range(EXPERTS_PER_GROUP):
        v = lrow(EXPERT_ROW0 + e)
        for g in range(1, N_EXPERT_GROUPS):
            v = jnp.where(gidx == g, lrow(EXPERT_ROW0 + g * EXPERTS_PER_GROUP + e), v)
        el.append(v)

    def first_argmax(vals):
        m = functools.reduce(jnp.maximum, vals)
        idx = jnp.full((1, tm), len(vals) - 1, I32)
        for e in range(len(vals) - 2, -1, -1):
            idx = jnp.where(vals[e] == m, e, idx)
        return m, idx

    m1, i1 = first_argmax(el)
    el2 = [jnp.where(i1 == e, -jnp.inf, el[e]) for e in range(EXPERTS_PER_GROUP)]
    m2, i2 = first_argmax(el2)
    e21 = jnp.exp(m2 - m1)
    w1 = g_weight / (1.0 + e21)
    w2 = g_weight * e21 / (1.0 + e21)
    eid1 = gidx * EXPERTS_PER_GROUP + i1
    eid2 = gidx * EXPERTS_PER_GROUP + i2

    erow = lax.broadcasted_iota(I32, (N_EXPERTS, tm), 0)
    oh1 = erow == eid1
    oh2 = erow == eid2
    member = jnp.where(oh1 | oh2, 1.0, 0.0)
    s_i = lax.broadcasted_iota(I32, (tm, tm), 0)
    t_i = lax.broadcasted_iota(I32, (tm, tm), 1)
    before = jnp.where(s_i < t_i, 1.0, 0.0).astype(BF16)
    excl = jnp.dot(member.astype(BF16), before, preferred_element_type=F32)
    carry = carry_ref[...]
    tot = excl + carry[:, 0:1]
    rank1 = jnp.sum(jnp.where(oh1, tot, 0.0), axis=0, keepdims=True)
    rank2 = jnp.sum(jnp.where(oh2, tot, 0.0), axis=0, keepdims=True)
    carry = carry + jnp.sum(member, axis=1, keepdims=True)
    carry_ref[...] = carry
    cnt_ref[...] = carry.astype(I32)

    idx_ref[...] = jnp.zeros_like(idx_ref)
    idx_ref[0:1, :] = eid1
    idx_ref[1:2, :] = eid2
    idx_ref[2:3, :] = rank1.astype(I32)
    idx_ref[3:4, :] = rank2.astype(I32)
    gw_ref[...] = jnp.zeros_like(gw_ref)
    gw_ref[0:1, :] = w1
    gw_ref[1:2, :] = w2


def _route(h, nw, w_group, b_group, w_router, b_router):
    t, d = h.shape
    tm = min(ROUTE_TM, t)
    wt = jnp.zeros((ROUTE_ROWS, d), F32)
    wt = wt.at[0:N_EXPERT_GROUPS].set(w_group.T)
    wt = wt.at[EXPERT_ROW0:EXPERT_ROW0 + N_EXPERTS].set(
        jnp.transpose(w_router, (0, 2, 1)).reshape(N_EXPERTS, d))
    bias = jnp.zeros((ROUTE_ROWS, 1), F32)
    bias = bias.at[0:N_EXPERT_GROUPS, 0].set(b_group)
    bias = bias.at[EXPERT_ROW0:EXPERT_ROW0 + N_EXPERTS, 0].set(b_router.reshape(N_EXPERTS))
    return pl.pallas_call(
        _route_kernel,
        grid=(t // tm,),
        in_specs=[
            pl.BlockSpec((tm, d), lambda i: (i, 0)),
            pl.BlockSpec((1, d), lambda i: (0, 0)),
            pl.BlockSpec((ROUTE_ROWS, d), lambda i: (0, 0)),
            pl.BlockSpec((ROUTE_ROWS, 1), lambda i: (0, 0)),
        ],
        out_specs=[
            pl.BlockSpec((tm * SLAB, V7X_LANES), lambda i: (i, 0)),
            pl.BlockSpec((V7X_SUBLANES, tm), lambda i: (0, i)),
            pl.BlockSpec((V7X_SUBLANES, tm), lambda i: (0, i)),
            pl.BlockSpec((N_EXPERTS, V7X_LANES), lambda i: (0, 0)),
        ],
        out_shape=[
            jax.ShapeDtypeStruct((t * SLAB, V7X_LANES), U32),
            jax.ShapeDtypeStruct((V7X_SUBLANES, t), I32),
            jax.ShapeDtypeStruct((V7X_SUBLANES, t), F32),
            jax.ShapeDtypeStruct((N_EXPERTS, V7X_LANES), I32),
        ],
        scratch_shapes=[pltpu.VMEM((N_EXPERTS, V7X_LANES), F32)],
        compiler_params=_cparams(1, 40),
        name="moe_route",
    )(h, nw.reshape(1, d), wt, bias)


def _slab_at(r):
    return pl.ds(pl.multiple_of(r * SLAB, SLAB), SLAB)


def _expert_kernel(src_ref, te_ref, nv_ref, hn_hbm, wg_ref, wu_ref, wd_ref, ys_ref,
                   xbuf, wg_b, wu_b, wd_b, gsems):
    i = pl.program_id(0)
    n_tiles = pl.num_programs(0) - 1
    n_live = nv_ref[0]
    live = i < n_live
    tm = EXPERT_TM

    def gather_copy(tile_slot, r, tok):
        return pltpu.make_async_copy(hn_hbm.at[_slab_at(tok)], xbuf.at[tile_slot, _slab_at(r)],
                                     gsems.at[tile_slot])

    def start_gather_row(virtual_tile, r, priority):
        base = jnp.minimum(virtual_tile, n_live - 1) * tm
        gather_copy(virtual_tile % GATHER_SLOTS, r, src_ref[base + r]).start(priority=priority)

    def burst(b):
        for r in range(b * BURST_ROWS, (b + 1) * BURST_ROWS):
            start_gather_row(i + GATHER_SLOTS - 1, r, r % 2)

    def gather_wait(tile_slot):
        def body(r, carry):
            gather_copy(tile_slot, 0, 0).wait()
            return carry

        lax.fori_loop(0, tm, body, 0, unroll=DMA_UNROLL)

    @pl.when(i == 0)
    def _():
        for v in range(GATHER_SLOTS - 1):
            def body(rr, carry, v=v):
                for u in range(DMA_UNROLL):
                    start_gather_row(v, rr * DMA_UNROLL + u, u % 2)
                return carry

            lax.fori_loop(0, tm // DMA_UNROLL, body, 0)

    changed = jnp.logical_or(i == 0, te_ref[i] != te_ref[jnp.maximum(i - 1, 0)])

    @pl.when(jnp.logical_and(live, changed))
    def _():
        wg_b[...] = wg_ref[0, 0].astype(BF16)
        wu_b[...] = wu_ref[0, 0].astype(BF16)
        wd_b[...] = wd_ref[0, 0].astype(BF16)

    @pl.when(live)
    def _():
        gather_wait(i % GATHER_SLOTS)
        x = jnp.concatenate([c.astype(BF16) for c in _slab_load(xbuf.at[i % GATHER_SLOTS], tm)],
                            axis=1)
        f = wg_b.shape[1]
        d = wd_b.shape[1]
        n_up = N_BURSTS // 4
        n_down = N_BURSTS // 2
        b = 0
        gate, up = [], []
        for w_b, parts in ((wg_b, gate), (wu_b, up)):
            for c in range(n_up):
                burst(b)
                b += 1
                cols = slice(c * (f // n_up), (c + 1) * (f // n_up))
                parts.append(jnp.dot(x, w_b[:, cols], preferred_element_type=F32))
        act = (_silu(jnp.concatenate(gate, axis=1)) * jnp.concatenate(up, axis=1)).astype(BF16)
        y = []
        for c in range(n_down):
            burst(b)
            b += 1
            cols = slice(c * (d // n_down), (c + 1) * (d // n_down))
            y.append(jnp.dot(act, wd_b[:, cols], preferred_element_type=F32))

        _slab_store(ys_ref, jnp.concatenate(y, axis=1))

    @pl.when(jnp.logical_and(jnp.logical_not(live), i < n_tiles))
    def _():
        ys_ref[...] = jnp.zeros_like(ys_ref)

    @pl.when(i == n_live)
    def _():
        gather_wait(i % GATHER_SLOTS)
        gather_wait((i + 1) % GATHER_SLOTS)


def _experts(hn, src, tile_expert, n_live, w_gate, w_up, w_down, layer):
    d = D_MODEL
    n_slots = src.shape[0]
    n_tiles = n_slots // EXPERT_TM
    f = w_gate.shape[3]

    def wsel(i, src_r, te, nv):
        return (layer, te[i], 0, 0)

    return pl.pallas_call(
        _expert_kernel,
        grid_spec=pltpu.PrefetchScalarGridSpec(
            num_scalar_prefetch=3,
            grid=(n_tiles + 1,),
            in_specs=[
                pl.BlockSpec(memory_space=pl.ANY),
                pl.BlockSpec((1, 1, d, f), wsel),
                pl.BlockSpec((1, 1, d, f), wsel),
                pl.BlockSpec((1, 1, f, d), wsel),
            ],
            out_specs=pl.BlockSpec((EXPERT_TM * SLAB, V7X_LANES),
                                   lambda i, src_r, te, nv: (jnp.minimum(i, n_tiles - 1), 0)),
            scratch_shapes=[pltpu.VMEM((GATHER_SLOTS, EXPERT_TM * SLAB, V7X_LANES), U32),
                            pltpu.VMEM((d, f), BF16), pltpu.VMEM((d, f), BF16),
                            pltpu.VMEM((f, d), BF16),
                            pltpu.SemaphoreType.DMA((GATHER_SLOTS,))],
        ),
        out_shape=jax.ShapeDtypeStruct((n_slots * SLAB, V7X_LANES), U32),
        compiler_params=_cparams(1, 52),
        name="moe_experts",
    )(src, tile_expert, n_live, hn, w_gate, w_up, w_down)


def _combine_kernel(pos_ref, h_ref, w1_ref, w2_ref, fw_ref, ys_hbm, o_ref, ybuf, sems,
                    *, n_tok, final_norm):
    i = pl.program_id(0)
    tm = h_ref.shape[0]

    def row_copy(buf, k, r, src_row):
        return pltpu.make_async_copy(ys_hbm.at[_slab_at(src_row)], ybuf.at[buf, k, _slab_at(r)],
                                     sems.at[buf])

    def gather_start(step):
        base = step * tm

        def body(rr, carry):
            for u in range(DMA_UNROLL):
                r = rr * DMA_UNROLL + u
                row_copy(step % 2, 0, r, pos_ref[base + r]).start(priority=u % 2)
                row_copy(step % 2, 1, r, pos_ref[n_tok + base + r]).start(priority=u % 2)
            return carry

        lax.fori_loop(0, tm // DMA_UNROLL, body, 0)

    def gather_wait(buf):
        def body(r, carry):
            row_copy(buf, 0, 0, 0).wait()
            return carry

        lax.fori_loop(0, 2 * tm, body, 0, unroll=DMA_UNROLL)

    @pl.when(i == 0)
    def _():
        gather_start(i)

    @pl.when(i + 1 < pl.num_programs(0))
    def _():
        gather_start(i + 1)

    gather_wait(i % 2)
    w1 = w1_ref[...]
    w2 = w2_ref[...]
    ssq = jnp.zeros((tm, 1), F32)
    y1 = _slab_load(ybuf.at[i % 2, 0], tm)
    y2 = _slab_load(ybuf.at[i % 2, 1], tm)
    for j in range(2 * SLAB):
        cols = slice(j * V7X_LANES, (j + 1) * V7X_LANES)
        out = h_ref[:, cols] + w1 * y1[j] + w2 * y2[j]
        o_ref[:, cols] = out
        ssq = ssq + jnp.sum(out * out, axis=-1, keepdims=True)
    if final_norm:
        inv = lax.rsqrt(ssq / D_MODEL + EPS)
        o_ref[...] = (o_ref[...] * inv) * fw_ref[...]


def _combine(h, ys, pos, w1, w2, final_w, final_norm):
    t, d = h.shape
    tm = min(COMBINE_TM, t)
    return pl.pallas_call(
        functools.partial(_combine_kernel, n_tok=t, final_norm=final_norm),
        grid_spec=pltpu.PrefetchScalarGridSpec(
            num_scalar_prefetch=1,
            grid=(t // tm,),
            in_specs=[
                pl.BlockSpec((tm, d), lambda i, p: (i, 0)),
                pl.BlockSpec((tm, 1), lambda i, p: (i, 0)),
                pl.BlockSpec((tm, 1), lambda i, p: (i, 0)),
                pl.BlockSpec((1, d), lambda i, p: (0, 0)),
                pl.BlockSpec(memory_space=pl.ANY),
            ],
            out_specs=pl.BlockSpec((tm, d), lambda i, p: (i, 0)),
            scratch_shapes=[pltpu.VMEM((2, 2, tm * SLAB, V7X_LANES), U32),
                            pltpu.SemaphoreType.DMA((2,))],
        ),
        out_shape=jax.ShapeDtypeStruct((t, d), F32),
        compiler_params=_cparams(1, 32),
        name="moe_combine",
    )(pos, h, w1.reshape(t, 1), w2.reshape(t, 1), final_w.reshape(1, d), ys)


def _moe(h, nw, w_group, b_group, w_router, b_router, w_gate, w_up, w_down, layer, final_w,
         final_norm):
    t, d = h.shape
    hn, idx, gw, cnt = _route(h, nw, w_group, b_group, w_router, b_router)
    counts = cnt[:, 0]
    padded = ((counts + EXPERT_TM - 1) // EXPERT_TM) * EXPERT_TM
    seg_end = jnp.cumsum(padded)
    seg_start = seg_end - padded
    pos = jnp.concatenate([seg_start[idx[0]] + idx[2], seg_start[idx[1]] + idx[3]]).astype(I32)
    n_tiles = (2 * t) // EXPERT_TM + N_EXPERTS
    n_slots = n_tiles * EXPERT_TM
    n_live = (seg_end[-1] // EXPERT_TM).astype(I32)
    tile_first = jnp.minimum(jnp.arange(n_tiles + 1, dtype=I32), n_live - 1) * EXPERT_TM
    tile_expert = jnp.sum(tile_first[:, None] >= seg_end[None, :], axis=1).astype(I32)
    tok = jnp.arange(t, dtype=I32)
    src = jnp.zeros((n_slots,), I32).at[pos].set(jnp.concatenate([tok, tok]), unique_indices=True,
                                                 mode="promise_in_bounds")
    ys = _experts(hn, src, tile_expert, n_live.reshape(1), w_gate, w_up, w_down, layer)
    return _combine(h, ys, pos, gw[0], gw[1], final_w, final_norm)


def kernel(x, norm_mix_w, norm_ffn_w, final_norm_w, mix_w_in, pool_w, pool_scale, attn_sinks,
           mix_w_out, ssm_w_in, ssm_conv_w, ssm_conv_b, ssm_dt_bias, ssm_a_log, ssm_d,
           ssm_norm_w, ssm_w_out, moe_w_group, moe_b_group, moe_w_router, moe_b_router,
           moe_w_gate, moe_w_up, moe_w_down):
    batch, seq_len, d = x.shape
    h = x.reshape(batch * seq_len, d)

    proj = _norm_matmul(h, norm_mix_w[0], mix_w_in, MIX_IN_WIDTH, MIX_IN_WIDTH // 5, BF16)
    mixed = _pool_attn(proj, pool_w, pool_scale, attn_sinks, seq_len)
    h = _matmul_residual(mixed, mix_w_out, h, D_MODEL // 2)
    h = _moe(h, norm_ffn_w[0], moe_w_group[0], moe_b_group[0], moe_w_router[0], moe_b_router[0],
             moe_w_gate, moe_w_up, moe_w_down, 0, final_norm_w, False)

    ssm_w_in_t = jnp.swapaxes(ssm_w_in, 1, 2)
    zx = _norm_matmul(h, norm_mix_w[1], ssm_w_in_t, ZX_WIDTH, ZX_WIDTH // 10, BF16, True)
    dt_raw = _norm_matmul(h, norm_mix_w[1], ssm_w_in_t[:, ZX_WIDTH:, :], N_SSM_HEADS, N_SSM_HEADS,
                          F32, True)
    y = _ssd(zx, dt_raw, ssm_conv_w, ssm_conv_b, ssm_dt_bias, ssm_a_log, ssm_d, ssm_norm_w,
             batch, seq_len)
    h = _matmul_residual(y, ssm_w_out, h, D_MODEL // 4)
    h = _moe(h, norm_ffn_w[1], moe_w_group[1], moe_b_group[1], moe_w_router[1], moe_b_router[1],
             moe_w_gate, moe_w_up, moe_w_down, 1, final_norm_w, True)
    return h.reshape(batch, seq_len, d)
```

```python
import functools

import jax
import jax.numpy as jnp
from jax import lax
from jax.experimental import pallas as pl
from jax.experimental.pallas import tpu as pltpu

F32 = jnp.float32
BF16 = jnp.bfloat16
I32 = jnp.int32

D_MODEL = 2048
EPS = 1e-6
POOL_WINDOWS = (2, 4, 8, 16)
POOL_WIDTH = 1024
POOL_GROUP = 256
HEAD_DIM = 64
N_Q_HEADS = 16
N_KV_HEADS = 4
Q_PER_KV = 4
ATTN_WIDTH = 1024
KV_WIDTH = 256
WINDOW = 128
BLOCK = 128
MIX_IN_WIDTH = 2560
D_INNER = 4096
SSM_HEAD_DIM = 64
N_SSM_HEADS = 64
N_SSM_GROUPS = 8
HEADS_PER_GROUP = 8
D_STATE = 128
CONV_WIDTH = 4
CHUNK = 128
CONV_CH = D_INNER + 2 * N_SSM_GROUPS * D_STATE
ZX_WIDTH = D_INNER + CONV_CH
N_EXPERT_GROUPS = 4
EXPERTS_PER_GROUP = 4
N_EXPERTS = 16
D_FF_EXPERT = 512
SLOPES = tuple(float(2.0 ** (-8.0 * (i + 1) / N_Q_HEADS)) for i in range(N_Q_HEADS))

V7X_LANES = 128
V7X_SUBLANES = 8

PROJ_TM = 1024
NORM_ROWS = 128
ROUTE_TM = 512
EXPERT_TM = 256
COMBINE_TM = 256
ROUTE_ROWS = 32
EXPERT_ROW0 = 8
DMA_UNROLL = 32
GATHER_SLOTS = 3
SCALAR_UNROLL = 16
N_BURSTS = 8
BURST_ROWS = EXPERT_TM // N_BURSTS
HALF = D_MODEL // 2
SLAB = HALF // V7X_LANES


def _cparams(n_axes, vmem_mb):
    return pltpu.CompilerParams(
        dimension_semantics=("arbitrary",) * n_axes,
        vmem_limit_bytes=vmem_mb * 1024 * 1024,
    )


def _silu(x):
    half = 0.5 * x
    return half + half * jnp.tanh(half)


def _split_bf16(x):
    hi = x.astype(BF16)
    lo = (x - hi.astype(F32)).astype(BF16)
    return hi, lo


def _slab_rows(j, n):
    return pl.ds(j, n, stride=SLAB)


U32 = jnp.uint32


def _pack_bf16_pair(lo, hi):
    lo_bits = lax.bitcast_convert_type(lo.astype(BF16).astype(F32), U32)
    hi_bits = lax.bitcast_convert_type(hi.astype(BF16).astype(F32), U32)
    return (lo_bits >> 16) | hi_bits


def _unpack_bf16_pair(word):
    lo = lax.bitcast_convert_type(word << 16, F32)
    hi = lax.bitcast_convert_type(word & jnp.uint32(0xFFFF0000), F32)
    return lo, hi


def _slab_store(ref, x):
    n = x.shape[0]
    for j in range(SLAB):
        lo = x[:, j * V7X_LANES:(j + 1) * V7X_LANES]
        hi = x[:, HALF + j * V7X_LANES:HALF + (j + 1) * V7X_LANES]
        ref[_slab_rows(j, n), :] = _pack_bf16_pair(lo, hi)


def _slab_load(ref, n):
    pairs = [_unpack_bf16_pair(ref[_slab_rows(j, n), :]) for j in range(SLAB)]
    return [p[0] for p in pairs] + [p[1] for p in pairs]


def _norm_matmul_kernel(x_ref, nw_ref, w_ref, o_ref, xn_ref, *, w_transposed):
    tm = x_ref.shape[0]

    @pl.when(pl.program_id(1) == 0)
    def _():
        def body(r, carry):
            rows = pl.ds(pl.multiple_of(r * NORM_ROWS, NORM_ROWS), NORM_ROWS)
            x = x_ref[rows, :]
            var = jnp.mean(x * x, axis=-1, keepdims=True)
            xn_ref[rows, :] = ((x * lax.rsqrt(var + EPS)) * nw_ref[...]).astype(BF16)
            return carry

        lax.fori_loop(0, tm // NORM_ROWS, body, 0)

    contract = ((1,), (1,)) if w_transposed else ((1,), (0,))
    o_ref[...] = lax.dot_general(xn_ref[...], w_ref[0].astype(BF16), (contract, ((), ())),
                                 preferred_element_type=F32).astype(o_ref.dtype)


def _norm_matmul(x, nw, w, n_out, tn, out_dtype, w_transposed=False):
    t, d = x.shape
    tm = min(PROJ_TM, t)
    if w_transposed:
        w_spec = pl.BlockSpec((1, tn, d), lambda i, j: (0, j, 0))
    else:
        w_spec = pl.BlockSpec((1, d, tn), lambda i, j: (0, 0, j))
    return pl.pallas_call(
        functools.partial(_norm_matmul_kernel, w_transposed=w_transposed),
        grid=(t // tm, n_out // tn),
        in_specs=[
            pl.BlockSpec((tm, d), lambda i, j: (i, 0)),
            pl.BlockSpec((1, d), lambda i, j: (0, 0)),
            w_spec,
        ],
        out_specs=pl.BlockSpec((tm, tn), lambda i, j: (i, j)),
        out_shape=jax.ShapeDtypeStruct((t, n_out), out_dtype),
        scratch_shapes=[pltpu.VMEM((tm, d), BF16)],
        compiler_params=_cparams(2, 56),
        name="norm_matmul",
    )(x, nw.reshape(1, d), w)


def _matmul_res_kernel(a_ref, w_ref, r_ref, o_ref):
    o_ref[...] = r_ref[...] + jnp.dot(a_ref[...], w_ref[0].astype(BF16),
                                      preferred_element_type=F32)


def _matmul_residual(a, w, res, tn):
    t, k = a.shape
    n = w.shape[2]
    tm = min(PROJ_TM, t)
    return pl.pallas_call(
        _matmul_res_kernel,
        grid=(t // tm, n // tn),
        in_specs=[
            pl.BlockSpec((tm, k), lambda i, j: (i, 0)),
            pl.BlockSpec((1, k, tn), lambda i, j: (0, 0, j)),
            pl.BlockSpec((tm, tn), lambda i, j: (i, j)),
        ],
        out_specs=pl.BlockSpec((tm, tn), lambda i, j: (i, j)),
        out_shape=jax.ShapeDtypeStruct((t, n), F32),
        compiler_params=_cparams(2, 56),
        name="matmul_residual",
    )(a, w, res)


def _pool_attn_kernel(sinks_ref, u_ref, up_ref, q_ref, k_ref, kp_ref, v_ref, vp_ref,
                      pw_ref, ps_ref, o_ref, *, blocks_per_seq):
    blk = pl.program_id(0) % blocks_per_seq
    first = blk == 0
    row = lax.broadcasted_iota(I32, (BLOCK, 2 * BLOCK), 0)
    col = lax.broadcasted_iota(I32, (BLOCK, 2 * BLOCK), 1)
    dist = row + BLOCK - col

    u_cur = u_ref[...]
    u_prev = jnp.where(first, jnp.zeros_like(u_cur), up_ref[...])
    u_ext = jnp.concatenate([u_prev, u_cur], axis=0)
    pos = blk * BLOCK + lax.broadcasted_iota(I32, (BLOCK, 1), 0)
    for g, w in enumerate(POOL_WINDOWS):
        cols = slice(g * POOL_GROUP, (g + 1) * POOL_GROUP)
        band = jnp.where((dist >= 0) & (dist < w), 1.0, 0.0).astype(BF16)
        wsum = jnp.dot(band, u_ext[:, cols], preferred_element_type=F32)
        count = jnp.minimum(pos + 1, w).astype(F32)
        pooled = (wsum / count - u_cur[:, cols].astype(F32)).astype(BF16)
        out_g = jnp.dot(pooled, pw_ref[0, g].astype(BF16), preferred_element_type=F32)
        o_ref[:, cols] = (out_g * ps_ref[:, cols]).astype(o_ref.dtype)

    kk = jnp.concatenate([kp_ref[...], k_ref[...]], axis=0)
    vv = jnp.concatenate([vp_ref[...], v_ref[...]], axis=0)
    valid = (dist >= 0) & (dist < WINDOW) & ((col >= BLOCK) | jnp.logical_not(first))
    dist_f = dist.astype(F32)
    scale = HEAD_DIM ** -0.5
    for pair in range(N_Q_HEADS // 2):
        q_pair = q_ref[:, pair * 2 * HEAD_DIM:(pair + 1) * 2 * HEAD_DIM]
        outs = []
        for sub in range(2):
            h = pair * 2 + sub
            g = h // Q_PER_KV
            qh = q_pair[:, sub * HEAD_DIM:(sub + 1) * HEAD_DIM]
            kg = kk[:, g * HEAD_DIM:(g + 1) * HEAD_DIM]
            vg = vv[:, g * HEAD_DIM:(g + 1) * HEAD_DIM]
            s = lax.dot_general(qh, kg, (((1,), (1,)), ((), ())),
                                preferred_element_type=F32) * scale
            s = jnp.where(valid, s - SLOPES[h] * dist_f, -jnp.inf)
            sink = sinks_ref[0, h]
            m = jnp.maximum(jnp.max(s, axis=-1, keepdims=True), sink)
            p = jnp.exp(s - m)
            denom = jnp.sum(p, axis=-1, keepdims=True) + jnp.exp(sink - m)
            probs = (p / denom).astype(BF16)
            outs.append(jnp.dot(probs, vg, preferred_element_type=F32))
        lo = ATTN_WIDTH + pair * 2 * HEAD_DIM
        o_ref[:, lo:lo + 2 * HEAD_DIM] = jnp.concatenate(outs, axis=1).astype(o_ref.dtype)


def _pool_attn(proj, pool_w, pool_scale, sinks, seq_len):
    t = proj.shape[0]
    nblk = t // BLOCK
    kcol = (POOL_WIDTH + ATTN_WIDTH) // KV_WIDTH

    def cur(cb):
        return lambda i: (i, cb)

    def prev(cb):
        return lambda i: (jnp.maximum(i - 1, 0), cb)

    return pl.pallas_call(
        functools.partial(_pool_attn_kernel, blocks_per_seq=seq_len // BLOCK),
        grid=(nblk,),
        in_specs=[
            pl.BlockSpec(memory_space=pltpu.SMEM),
            pl.BlockSpec((BLOCK, POOL_WIDTH), cur(0)),
            pl.BlockSpec((BLOCK, POOL_WIDTH), prev(0)),
            pl.BlockSpec((BLOCK, ATTN_WIDTH), cur(1)),
            pl.BlockSpec((BLOCK, KV_WIDTH), cur(kcol)),
            pl.BlockSpec((BLOCK, KV_WIDTH), prev(kcol)),
            pl.BlockSpec((BLOCK, KV_WIDTH), cur(kcol + 1)),
            pl.BlockSpec((BLOCK, KV_WIDTH), prev(kcol + 1)),
            pl.BlockSpec((1, len(POOL_WINDOWS), POOL_GROUP, POOL_GROUP), lambda i: (0, 0, 0, 0)),
            pl.BlockSpec((1, POOL_WIDTH), lambda i: (0, 0)),
        ],
        out_specs=pl.BlockSpec((BLOCK, POOL_WIDTH + ATTN_WIDTH), lambda i: (i, 0)),
        out_shape=jax.ShapeDtypeStruct((t, POOL_WIDTH + ATTN_WIDTH), BF16),
        compiler_params=_cparams(1, 32),
        name="pool_attn",
    )(sinks, proj, proj, proj, proj, proj, proj, proj, pool_w, pool_scale)


def _ssd_kernel(z_ref, x_ref, bc_ref, dt_ref, cw_ref, cb_ref, dtb_ref, alog_ref, dexp_ref,
                nw_ref, e2_ref, o_ref, prev_ref, state_ref, y_ref):
    gn = N_SSM_GROUPS * D_STATE
    taps = CONV_WIDTH - 1

    @pl.when(pl.program_id(1) == 0)
    def _():
        prev_ref[...] = jnp.zeros_like(prev_ref)
        state_ref[...] = jnp.zeros_like(state_ref)

    cur = jnp.concatenate([x_ref[...], bc_ref[...]], axis=1)
    cat = jnp.concatenate([prev_ref[...], cur], axis=0)
    prev_ref[...] = cur
    srow = lax.broadcasted_iota(I32, (taps * CHUNK, 2 * CHUNK), 0)
    scol = lax.broadcasted_iota(I32, (taps * CHUNK, 2 * CHUNK), 1)
    shift = jnp.where(scol + taps - srow // CHUNK == srow % CHUNK + CHUNK, 1.0, 0.0).astype(BF16)
    shifted = jnp.dot(shift, cat, preferred_element_type=F32)
    conv = cb_ref[...] + cw_ref[taps:taps + 1, :] * cur.astype(F32)
    for j in range(taps):
        conv = conv + cw_ref[j:j + 1, :] * shifted[j * CHUNK:(j + 1) * CHUNK, :]
    act = _silu(conv)
    xs = act[:, 0:D_INNER]
    xs_b = xs.astype(BF16)
    bm = act[:, D_INNER:D_INNER + gn]
    cm = act[:, D_INNER + gn:CONV_CH]

    dt_in = dt_ref[...] + dtb_ref[...]
    dt = jnp.maximum(dt_in, 0.0) + jnp.log(1.0 + jnp.exp(-jnp.abs(dt_in)))
    da = dt * (-jnp.exp(alog_ref[...]))
    row = lax.broadcasted_iota(I32, (CHUNK, CHUNK), 0)
    col = lax.broadcasted_iota(I32, (CHUNK, CHUNK), 1)
    causal = row >= col
    tri = jnp.where(causal, 1.0, 0.0).astype(BF16)
    da_hi, da_lo = _split_bf16(da)
    a_cum = (jnp.dot(tri, da_hi, preferred_element_type=F32)
             + jnp.dot(tri, da_lo, preferred_element_type=F32))
    e_cum = jnp.exp(a_cum)
    tr = jnp.concatenate([a_cum, dt], axis=1).T
    a_cum_t = tr[0:N_SSM_HEADS, :]
    dt_t = tr[N_SSM_HEADS:2 * N_SSM_HEADS, :]
    a_last_col = a_cum_t[:, CHUNK - 1:CHUNK]
    w_t = dt_t * jnp.exp(a_last_col - a_cum_t)
    dec8 = jnp.broadcast_to(e_cum[CHUNK - 1:CHUNK, :], (V7X_SUBLANES, N_SSM_HEADS))
    dec_hi, dec_lo = _split_bf16(dec8)
    dec_exp = (jnp.dot(dec_hi, e2_ref[...], preferred_element_type=F32)
               + jnp.dot(dec_lo, e2_ref[...], preferred_element_type=F32))[0:1, :]

    lane = lax.broadcasted_iota(I32, (CHUNK, 2 * SSM_HEAD_DIM), 1)
    left = lane < SSM_HEAD_DIM
    for g in range(N_SSM_GROUPS):
        b_g = bm[:, g * D_STATE:(g + 1) * D_STATE]
        c_g = cm[:, g * D_STATE:(g + 1) * D_STATE]
        bt_g = b_g.T
        cb_g = lax.dot_general(c_g.astype(BF16), b_g.astype(BF16),
                               (((1,), (1,)), ((), ())), preferred_element_type=F32)
        for k in range(HEADS_PER_GROUP // 2):
            slab = slice((g * 4 + k) * 2 * SSM_HEAD_DIM, (g * 4 + k + 1) * 2 * SSM_HEAD_DIM)
            lhs_y = []
            lhs_s = []
            for sub in range(2):
                h = g * HEADS_PER_GROUP + 2 * k + sub
                colb = jnp.broadcast_to(a_cum[:, h:h + 1], (CHUNK, CHUNK))
                rowb = jnp.broadcast_to(a_cum_t[h:h + 1, :], (CHUNK, CHUNK))
                decay = jnp.exp(jnp.where(causal, colb - rowb, -jnp.inf))
                m_h = cb_g * decay * jnp.broadcast_to(dt_t[h:h + 1, :], (CHUNK, CHUNK))
                lhs_y.append(m_h.astype(BF16))
                lhs_y.append((c_g * jnp.broadcast_to(e_cum[:, h:h + 1], (CHUNK, CHUNK))).astype(BF16))
                lhs_s.append((bt_g * jnp.broadcast_to(w_t[h:h + 1, :], (CHUNK, CHUNK))).astype(BF16))
            x_slab = xs_b[:, slab]
            s_slab = state_ref[:, slab]
            s_slab_b = s_slab.astype(BF16)
            zero = jnp.zeros_like(x_slab)
            x_l = jnp.where(left, x_slab, zero)
            x_r = jnp.where(left, zero, x_slab)
            s_l = jnp.where(left, s_slab_b, zero)
            s_r = jnp.where(left, zero, s_slab_b)
            y_pair = jnp.dot(jnp.concatenate(lhs_y, axis=1),
                             jnp.concatenate([x_l, s_l, x_r, s_r], axis=0),
                             preferred_element_type=F32)
            y_ref[:, slab] = y_pair
            d_state = jnp.dot(jnp.concatenate(lhs_s, axis=1),
                              jnp.concatenate([x_l, x_r], axis=0),
                              preferred_element_type=F32)
            state_ref[:, slab] = s_slab * dec_exp[:, slab] + d_state

    gsz = D_INNER // N_SSM_GROUPS
    for g in range(N_SSM_GROUPS):
        cols = slice(g * gsz, (g + 1) * gsz)
        y = y_ref[:, cols] + dexp_ref[:, cols] * xs[:, cols]
        gated = y * _silu(z_ref[:, cols].astype(F32))
        var = jnp.mean(gated * gated, axis=-1, keepdims=True)
        o_ref[:, cols] = ((gated * lax.rsqrt(var + EPS)) * nw_ref[:, cols]).astype(o_ref.dtype)


def _ssd(zx, dt_raw, conv_w, conv_b, dt_bias, a_log, d_skip, norm_w, batch, seq_len):
    t = zx.shape[0]
    nc = seq_len // CHUNK
    d_exp = jnp.repeat(d_skip, SSM_HEAD_DIM, axis=1)
    e2 = jnp.repeat(jnp.eye(N_SSM_HEADS, dtype=BF16), SSM_HEAD_DIM, axis=1)
    bc_col = (2 * D_INNER) // (2 * N_SSM_GROUPS * D_STATE)

    def rows(cb):
        return lambda b, c: (b * nc + c, cb)

    def whole(b, c):
        return (0, 0)

    return pl.pallas_call(
        _ssd_kernel,
        grid=(batch, nc),
        in_specs=[
            pl.BlockSpec((CHUNK, D_INNER), rows(0)),
            pl.BlockSpec((CHUNK, D_INNER), rows(1)),
            pl.BlockSpec((CHUNK, 2 * N_SSM_GROUPS * D_STATE), rows(bc_col)),
            pl.BlockSpec((CHUNK, N_SSM_HEADS), rows(0)),
            pl.BlockSpec((None, CONV_WIDTH, CONV_CH), lambda b, c: (0, 0, 0)),
            pl.BlockSpec((1, CONV_CH), whole),
            pl.BlockSpec((1, N_SSM_HEADS), whole),
            pl.BlockSpec((1, N_SSM_HEADS), whole),
            pl.BlockSpec((1, D_INNER), whole),
            pl.BlockSpec((1, D_INNER), whole),
            pl.BlockSpec((N_SSM_HEADS, D_INNER), whole),
        ],
        out_specs=pl.BlockSpec((CHUNK, D_INNER), rows(0)),
        out_shape=jax.ShapeDtypeStruct((t, D_INNER), BF16),
        scratch_shapes=[
            pltpu.VMEM((CHUNK, CONV_CH), BF16),
            pltpu.VMEM((D_STATE, D_INNER), F32),
            pltpu.VMEM((CHUNK, D_INNER), F32),
        ],
        compiler_params=_cparams(2, 48),
        name="ssd",
    )(zx, zx, zx, dt_raw, conv_w, conv_b, dt_bias, a_log, d_exp, norm_w, e2)


def _route_kernel(h_ref, nw_ref, wt_ref, bias_ref, hn_ref, idx_ref, gw_ref, cnt_ref, carry_ref):
    tm = h_ref.shape[0]

    @pl.when(pl.program_id(0) == 0)
    def _():
        carry_ref[...] = jnp.zeros_like(carry_ref)

    h = h_ref[...]
    var = jnp.mean(h * h, axis=-1, keepdims=True)
    hn = (h * lax.rsqrt(var + EPS)) * nw_ref[...]
    _slab_store(hn_ref, hn)

    nt = (((1,), (1,)), ((), ()))
    hn_hi, hn_lo = _split_bf16(hn)
    w_hi, w_lo = _split_bf16(wt_ref[...])
    logits = (lax.dot_general(w_hi, hn_hi, nt, preferred_element_type=F32)
              + lax.dot_general(w_hi, hn_lo, nt, preferred_element_type=F32)
              + lax.dot_general(w_lo, hn_hi, nt, preferred_element_type=F32)
              + bias_ref[...])

    def lrow(r):
        return logits[r:r + 1, :]

    gl = [lrow(g) for g in range(N_EXPERT_GROUPS)]
    gmax = functools.reduce(jnp.maximum, gl)
    gidx = jnp.full((1, tm), N_EXPERT_GROUPS - 1, I32)
    for g in range(N_EXPERT_GROUPS - 2, -1, -1):
        gidx = jnp.where(gl[g] == gmax, g, gidx)
    gsum = functools.reduce(jnp.add, [jnp.exp(x - gmax) for x in gl])
    g_weight = 1.0 / gsum

    el = []
    for e in range(EXPERTS_PER_GROUP):
        v = lrow(EXPERT_ROW0 + e)
        for g in range(1, N_EXPERT_GROUPS):
            v = jnp.where(gidx == g, lrow(EXPERT_ROW0 + g * EXPERTS_PER_GROUP + e), v)
        el.append(v)

    def first_argmax(vals):
        m = functools.reduce(jnp.maximum, vals)
        idx = jnp.full((1, tm), len(vals) - 1, I32)
        for e in range(len(vals) - 2, -1, -1):
            idx = jnp.where(vals[e] == m, e, idx)
        return m, idx

    m1, i1 = first_argmax(el)
    el2 = [jnp.where(i1 == e, -jnp.inf, el[e]) for e in range(EXPERTS_PER_GROUP)]
    m2, i2 = first_argmax(el2)
    e21 = jnp.exp(m2 - m1)
    w1 = g_weight / (1.0 + e21)
    w2 = g_weight * e21 / (1.0 + e21)
    eid1 = gidx * EXPERTS_PER_GROUP + i1
    eid2 = gidx * EXPERTS_PER_GROUP + i2

    erow = lax.broadcasted_iota(I32, (N_EXPERTS, tm), 0)
    oh1 = erow == eid1
    oh2 = erow == eid2
    member = jnp.where(oh1 | oh2, 1.0, 0.0)
    s_i = lax.broadcasted_iota(I32, (tm, tm), 0)
    t_i = lax.broadcasted_iota(I32, (tm, tm), 1)
    before = jnp.where(s_i < t_i, 1.0, 0.0).astype(BF16)
    excl = jnp.dot(member.astype(BF16), before, preferred_element_type=F32)
    carry = carry_ref[...]
    tot = excl + carry[:, 0:1]
    rank1 = jnp.sum(jnp.where(oh1, tot, 0.0), axis=0, keepdims=True)
    rank2 = jnp.sum(jnp.where(oh2, tot, 0.0), axis=0, keepdims=True)
    carry = carry + jnp.sum(member, axis=1, keepdims=True)
    carry_ref[...] = carry
    cnt_ref[...] = carry.astype(I32)

    idx_ref[...] = jnp.zeros_like(idx_ref)
    idx_ref[0:1, :] = eid1
    idx_ref[1:2, :] = eid2
    idx_ref[2:3, :] = rank1.astype(I32)
    idx_ref[3:4, :] = rank2.astype(I32)
    gw_ref[...] = jnp.zeros_like(gw_ref)
    gw_ref[0:1, :] = w1
    gw_ref[1:2, :] = w2


def _route(h, nw, w_group, b_group, w_router, b_router):
    t, d = h.shape
    tm = min(ROUTE_TM, t)
    wt = jnp.zeros((ROUTE_ROWS, d), F32)
    wt = wt.at[0:N_EXPERT_GROUPS].set(w_group.T)
    wt = wt.at[EXPERT_ROW0:EXPERT_ROW0 + N_EXPERTS].set(
        jnp.transpose(w_router, (0, 2, 1)).reshape(N_EXPERTS, d))
    bias = jnp.zeros((ROUTE_ROWS, 1), F32)
    bias = bias.at[0:N_EXPERT_GROUPS, 0].set(b_group)
    bias = bias.at[EXPERT_ROW0:EXPERT_ROW0 + N_EXPERTS, 0].set(b_router.reshape(N_EXPERTS))
    return pl.pallas_call(
        _route_kernel,
        grid=(t // tm,),
        in_specs=[
            pl.BlockSpec((tm, d), lambda i: (i, 0)),
            pl.BlockSpec((1, d), lambda i: (0, 0)),
            pl.BlockSpec((ROUTE_ROWS, d), lambda i: (0, 0)),
            pl.BlockSpec((ROUTE_ROWS, 1), lambda i: (0, 0)),
        ],
        out_specs=[
            pl.BlockSpec((tm * SLAB, V7X_LANES), lambda i: (i, 0)),
            pl.BlockSpec((V7X_SUBLANES, tm), lambda i: (0, i)),
            pl.BlockSpec((V7X_SUBLANES, tm), lambda i: (0, i)),
            pl.BlockSpec((N_EXPERTS, V7X_LANES), lambda i: (0, 0)),
        ],
        out_shape=[
            jax.ShapeDtypeStruct((t * SLAB, V7X_LANES), U32),
            jax.ShapeDtypeStruct((V7X_SUBLANES, t), I32),
            jax.ShapeDtypeStruct((V7X_SUBLANES, t), F32),
            jax.ShapeDtypeStruct((N_EXPERTS, V7X_LANES), I32),
        ],
        scratch_shapes=[pltpu.VMEM((N_EXPERTS, V7X_LANES), F32)],
        compiler_params=_cparams(1, 40),
        name="moe_route",
    )(h, nw.reshape(1, d), wt, bias)


def _slab_at(r):
    return pl.ds(pl.multiple_of(r * SLAB, SLAB), SLAB)


def _expert_kernel(pos_ref, te_ref, nv_ref, hn_hbm, wg_ref, wu_ref, wd_ref, ys_ref,
                   xbuf, wg_b, wu_b, wd_b, src_ref, gsems):
    i = pl.program_id(0)
    n_tiles = pl.num_programs(0) - 1
    n_live = nv_ref[0]
    live = i < n_live
    tm = EXPERT_TM

    def gather_copy(tile_slot, r, tok):
        return pltpu.make_async_copy(hn_hbm.at[_slab_at(tok)], xbuf.at[tile_slot, _slab_at(r)],
                                     gsems.at[tile_slot])

    def start_gather_row(virtual_tile, r, priority):
        base = jnp.minimum(virtual_tile, n_live - 1) * tm
        gather_copy(virtual_tile % GATHER_SLOTS, r, src_ref[base + r]).start(priority=priority)

    def burst(b):
        for r in range(b * BURST_ROWS, (b + 1) * BURST_ROWS):
            start_gather_row(i + GATHER_SLOTS - 1, r, r % 2)

    def gather_wait(tile_slot):
        def body(r, carry):
            gather_copy(tile_slot, 0, 0).wait()
            return carry

        lax.fori_loop(0, tm, body, 0, unroll=DMA_UNROLL)

    @pl.when(i == 0)
    def _():
        n_tok = pos_ref.shape[0] // 2

        def clear(j, carry):
            for u in range(SCALAR_UNROLL):
                src_ref[j * SCALAR_UNROLL + u] = 0
            return carry

        lax.fori_loop(0, src_ref.shape[0] // SCALAR_UNROLL, clear, 0)

        def fill(j, carry):
            for u in range(SCALAR_UNROLL):
                tok = j * SCALAR_UNROLL + u
                src_ref[pos_ref[tok]] = tok
                src_ref[pos_ref[n_tok + tok]] = tok
            return carry

        lax.fori_loop(0, n_tok // SCALAR_UNROLL, fill, 0)

        for v in range(GATHER_SLOTS - 1):
            def body(rr, carry, v=v):
                for u in range(DMA_UNROLL):
                    start_gather_row(v, rr * DMA_UNROLL + u, u % 2)
                return carry

            lax.fori_loop(0, tm // DMA_UNROLL, body, 0)

    changed = jnp.logical_or(i == 0, te_ref[i] != te_ref[jnp.maximum(i - 1, 0)])

    @pl.when(jnp.logical_and(live, changed))
    def _():
        wg_b[...] = wg_ref[0, 0].astype(BF16)
        wu_b[...] = wu_ref[0, 0].astype(BF16)
        wd_b[...] = wd_ref[0, 0].astype(BF16)

    @pl.when(live)
    def _():
        gather_wait(i % GATHER_SLOTS)
        x = jnp.concatenate([c.astype(BF16) for c in _slab_load(xbuf.at[i % GATHER_SLOTS], tm)],
                            axis=1)
        f = wg_b.shape[1]
        d = wd_b.shape[1]
        n_up = N_BURSTS // 4
        n_down = N_BURSTS // 2
        b = 0
        gate, up = [], []
        for w_b, parts in ((wg_b, gate), (wu_b, up)):
            for c in range(n_up):
                burst(b)
                b += 1
                cols = slice(c * (f // n_up), (c + 1) * (f // n_up))
                parts.append(jnp.dot(x, w_b[:, cols], preferred_element_type=F32))
        act = (_silu(jnp.concatenate(gate, axis=1)) * jnp.concatenate(up, axis=1)).astype(BF16)
        y = []
        for c in range(n_down):
            burst(b)
            b += 1
            cols = slice(c * (d // n_down), (c + 1) * (d // n_down))
            y.append(jnp.dot(act, wd_b[:, cols], preferred_element_type=F32))

        _slab_store(ys_ref, jnp.concatenate(y, axis=1))

    @pl.when(jnp.logical_and(jnp.logical_not(live), i < n_tiles))
    def _():
        ys_ref[...] = jnp.zeros_like(ys_ref)

    @pl.when(i == n_live)
    def _():
        gather_wait(i % GATHER_SLOTS)
        gather_wait((i + 1) % GATHER_SLOTS)


def _experts(hn, pos, n_slots, tile_expert, n_live, w_gate, w_up, w_down, layer):
    d = D_MODEL
    n_tiles = n_slots // EXPERT_TM
    f = w_gate.shape[3]

    def wsel(i, pos_r, te, nv):
        return (layer, te[i], 0, 0)

    return pl.pallas_call(
        _expert_kernel,
        grid_spec=pltpu.PrefetchScalarGridSpec(
            num_scalar_prefetch=3,
            grid=(n_tiles + 1,),
            in_specs=[
                pl.BlockSpec(memory_space=pl.ANY),
                pl.BlockSpec((1, 1, d, f), wsel),
                pl.BlockSpec((1, 1, d, f), wsel),
                pl.BlockSpec((1, 1, f, d), wsel),
            ],
            out_specs=pl.BlockSpec((EXPERT_TM * SLAB, V7X_LANES),
                                   lambda i, pos_r, te, nv: (jnp.minimum(i, n_tiles - 1), 0)),
            scratch_shapes=[pltpu.VMEM((GATHER_SLOTS, EXPERT_TM * SLAB, V7X_LANES), U32),
                            pltpu.VMEM((d, f), BF16), pltpu.VMEM((d, f), BF16),
                            pltpu.VMEM((f, d), BF16),
                            pltpu.SMEM((n_slots,), I32),
                            pltpu.SemaphoreType.DMA((GATHER_SLOTS,))],
        ),
        out_shape=jax.ShapeDtypeStruct((n_slots * SLAB, V7X_LANES), U32),
        compiler_params=_cparams(1, 52),
        name="moe_experts",
    )(pos, tile_expert, n_live, hn, w_gate, w_up, w_down)


def _combine_kernel(pos_ref, h_ref, w1_ref, w2_ref, fw_ref, ys_hbm, o_ref, ybuf, sems,
                    *, n_tok, final_norm):
    i = pl.program_id(0)
    tm = h_ref.shape[0]

    def row_copy(buf, k, r, src_row):
        return pltpu.make_async_copy(ys_hbm.at[_slab_at(src_row)], ybuf.at[buf, k, _slab_at(r)],
                                     sems.at[buf])

    def gather_start(step):
        base = step * tm

        def body(rr, carry):
            for u in range(DMA_UNROLL):
                r = rr * DMA_UNROLL + u
                row_copy(step % 2, 0, r, pos_ref[base + r]).start(priority=u % 2)
                row_copy(step % 2, 1, r, pos_ref[n_tok + base + r]).start(priority=u % 2)
            return carry

        lax.fori_loop(0, tm // DMA_UNROLL, body, 0)

    def gather_wait(buf):
        def body(r, carry):
            row_copy(buf, 0, 0, 0).wait()
            return carry

        lax.fori_loop(0, 2 * tm, body, 0, unroll=DMA_UNROLL)

    @pl.when(i == 0)
    def _():
        gather_start(i)

    @pl.when(i + 1 < pl.num_programs(0))
    def _():
        gather_start(i + 1)

    gather_wait(i % 2)
    w1 = w1_ref[...]
    w2 = w2_ref[...]
    ssq = jnp.zeros((tm, 1), F32)
    y1 = _slab_load(ybuf.at[i % 2, 0], tm)
    y2 = _slab_load(ybuf.at[i % 2, 1], tm)
    for j in range(2 * SLAB):
        cols = slice(j * V7X_LANES, (j + 1) * V7X_LANES)
        out = h_ref[:, cols] + w1 * y1[j] + w2 * y2[j]
        o_ref[:, cols] = out
        ssq = ssq + jnp.sum(out * out, axis=-1, keepdims=True)
    if final_norm:
        inv = lax.rsqrt(ssq / D_MODEL + EPS)
        o_ref[...] = (o_ref[...] * inv) * fw_ref[...]


def _combine(h, ys, pos, w1, w2, final_w, final_norm):
    t, d = h.shape
    tm = min(COMBINE_TM, t)
    return pl.pallas_call(
        functools.partial(_combine_kernel, n_tok=t, final_norm=final_norm),
        grid_spec=pltpu.PrefetchScalarGridSpec(
            num_scalar_prefetch=1,
            grid=(t // tm,),
            in_specs=[
                pl.BlockSpec((tm, d), lambda i, p: (i, 0)),
                pl.BlockSpec((tm, 1), lambda i, p: (i, 0)),
                pl.BlockSpec((tm, 1), lambda i, p: (i, 0)),
                pl.BlockSpec((1, d), lambda i, p: (0, 0)),
                pl.BlockSpec(memory_space=pl.ANY),
            ],
            out_specs=pl.BlockSpec((tm, d), lambda i, p: (i, 0)),
            scratch_shapes=[pltpu.VMEM((2, 2, tm * SLAB, V7X_LANES), U32),
                            pltpu.SemaphoreType.DMA((2,))],
        ),
        out_shape=jax.ShapeDtypeStruct((t, d), F32),
        compiler_params=_cparams(1, 32),
        name="moe_combine",
    )(pos, h, w1.reshape(t, 1), w2.reshape(t, 1), final_w.reshape(1, d), ys)


def _moe(h, nw, w_group, b_group, w_router, b_router, w_gate, w_up, w_down, layer, final_w,
         final_norm):
    t, d = h.shape
    hn, idx, gw, cnt = _route(h, nw, w_group, b_group, w_router, b_router)
    counts = cnt[:, 0]
    padded = ((counts + EXPERT_TM - 1) // EXPERT_TM) * EXPERT_TM
    seg_end = jnp.cumsum(padded)
    seg_start = seg_end - padded
    pos = jnp.concatenate([seg_start[idx[0]] + idx[2], seg_start[idx[1]] + idx[3]]).astype(I32)
    n_tiles = (2 * t) // EXPERT_TM + N_EXPERTS
    n_slots = n_tiles * EXPERT_TM
    n_live = (seg_end[-1] // EXPERT_TM).astype(I32)
    tile_first = jnp.minimum(jnp.arange(n_tiles + 1, dtype=I32), n_live - 1) * EXPERT_TM
    tile_expert = jnp.sum(tile_first[:, None] >= seg_end[None, :], axis=1).astype(I32)
    ys = _experts(hn, pos, n_slots, tile_expert, n_live.reshape(1), w_gate, w_up, w_down, layer)
    return _combine(h, ys, pos, gw[0], gw[1], final_w, final_norm)


def kernel(x, norm_mix_w, norm_ffn_w, final_norm_w, mix_w_in, pool_w, pool_scale, attn_sinks,
           mix_w_out, ssm_w_in, ssm_conv_w, ssm_conv_b, ssm_dt_bias, ssm_a_log, ssm_d,
           ssm_norm_w, ssm_w_out, moe_w_group, moe_b_group, moe_w_router, moe_b_router,
           moe_w_gate, moe_w_up, moe_w_down):
    batch, seq_len, d = x.shape
    h = x.reshape(batch * seq_len, d)

    proj = _norm_matmul(h, norm_mix_w[0], mix_w_in, MIX_IN_WIDTH, MIX_IN_WIDTH // 5, BF16)
    mixed = _pool_attn(proj, pool_w, pool_scale, attn_sinks, seq_len)
    h = _matmul_residual(mixed, mix_w_out, h, D_MODEL // 2)
    h = _moe(h, norm_ffn_w[0], moe_w_group[0], moe_b_group[0], moe_w_router[0], moe_b_router[0],
             moe_w_gate, moe_w_up, moe_w_down, 0, final_norm_w, False)

    ssm_w_in_t = jnp.swapaxes(ssm_w_in, 1, 2)
    zx = _norm_matmul(h, norm_mix_w[1], ssm_w_in_t, ZX_WIDTH, ZX_WIDTH // 10, BF16, True)
    dt_raw = _norm_matmul(h, norm_mix_w[1], ssm_w_in_t[:, ZX_WIDTH:, :], N_SSM_HEADS, N_SSM_HEADS,
                          F32, True)
    y = _ssd(zx, dt_raw, ssm_conv_w, ssm_conv_b, ssm_dt_bias, ssm_a_log, ssm_d, ssm_norm_w,
             batch, seq_len)
    h = _matmul_residual(y, ssm_w_out, h, D_MODEL // 4)
    h = _moe(h, norm_ffn_w[1], moe_w_group[1], moe_b_group[1], moe_w_router[1], moe_b_router[1],
             moe_w_gate, moe_w_up, moe_w_down, 1, final_norm_w, True)
    return h.reshape(batch, seq_len, d)
```
